```python
import math
import jax, jax.numpy as jnp
from jax import lax
import numpy as np

D_MODEL = 2048
BATCH = 1
SEQ = 8192
DEPTH = 2
DEC_BATCH = 128
DEC_SEQ = 8
PAST_LEN = 2048
PAGE_SIZE = 128

D_FF = 5632
H_A = 8
D_QK = 64
G_B = 8
C_B = 128
W_B = G_B * C_B
CHUNK_B = 128
H_C = 16
D_C = D_MODEL // H_C
DILATED_BRANCHES = ((128, 1), (512, 4), (2048, 16))
MAX_WINDOW = max(w for w, _ in DILATED_BRANCHES)
N_BUCKETS = 32
MAX_DISTANCE = 128
N_REL_HEADS = max(H_A, H_C)
Q_BLOCK = 128
EPS = 1e-6
NEG_INF = -1e30
W_QK_A = H_A * 2 * D_QK
W_AB_IN = 3 * W_QK_A + 2 * W_B
W_AB_OUT = W_QK_A + W_B
N_EVEN = (DEPTH + 1) // 2
N_ODD = DEPTH // 2

kernel_name = 'hybrid_diffattn_gmlp_dilated_step'


def rmsnorm(x, g):
    x32 = x.astype(jnp.float32)
    y = x32 * lax.rsqrt(jnp.mean(x32 * x32, axis=-1, keepdims=True) + EPS)
    return (y * g.astype(jnp.float32)).astype(x.dtype)


def layernorm(x, g, b):
    x32 = x.astype(jnp.float32)
    xc = x32 - jnp.mean(x32, axis=-1, keepdims=True)
    y = xc * lax.rsqrt(jnp.mean(xc * xc, axis=-1, keepdims=True) + EPS)
    return (y * g.astype(jnp.float32) + b.astype(jnp.float32)).astype(x.dtype)


def swiglu(x, wg, wu, wd):
    return (jax.nn.silu(x @ wg) * (x @ wu)) @ wd


def rel_bucket(dist):
    max_exact = N_BUCKETS // 2
    d = jnp.maximum(dist, 0)
    ratio = jnp.log(jnp.maximum(d, 1).astype(jnp.float32) / max_exact) / math.log(MAX_DISTANCE / max_exact)
    large = jnp.minimum(max_exact + (ratio * (N_BUCKETS - max_exact)).astype(jnp.int32), N_BUCKETS - 1)
    return jnp.where(d < max_exact, d, large)


def softmax_stats(s, mask):
    s = jnp.where(mask, s, NEG_INF)
    m = jnp.max(s, axis=-1, keepdims=True)
    e = jnp.exp(s - m)
    den = jnp.sum(e, axis=-1, keepdims=True)
    return e / den, (m + jnp.log(den))[..., 0]


def lambda_init(layer):
    return 0.8 - 0.6 * math.exp(-0.3 * layer)


def diff_lambda(lam_p, layer):
    lp = lam_p.astype(jnp.float32)
    return jnp.exp(jnp.sum(lp[0] * lp[1])) - jnp.exp(jnp.sum(lp[2] * lp[3])) + lambda_init(layer)


def diff_attend(q, k, v, q_pos, k_pos, lam, rel_bias):
    s = jnp.einsum('bqhie,bkhie->bhiqk', q, k).astype(jnp.float32) * (D_QK ** -0.5)
    dist = q_pos[:, None] - k_pos[None, :]
    bias = rel_bias[:, :H_A].astype(jnp.float32)[rel_bucket(dist)]
    s = s + jnp.transpose(bias, (2, 0, 1))[None, :, None]
    p = jax.nn.softmax(jnp.where(dist >= 0, s, NEG_INF), axis=-1)
    a = p[:, :, 0] - lam * p[:, :, 1]
    return jnp.einsum('bhqk,bkhe->bqhe', a.astype(v.dtype), v)


def project_ab(h, w_in, ln_g, ln_b):
    B, S, _ = h.shape
    z = h @ w_in
    q, k, v, u, g = jnp.split(z, [W_QK_A, 2 * W_QK_A, 3 * W_QK_A, 3 * W_QK_A + W_B], axis=-1)
    q = q.reshape(B, S, H_A, 2, D_QK)
    k = k.reshape(B, S, H_A, 2, D_QK)
    v = v.reshape(B, S, H_A, 2 * D_QK)
    u = jax.nn.gelu(u, approximate=False)
    g = layernorm(jax.nn.gelu(g, approximate=False), ln_g, ln_b)
    return q, k, v, u, g


def spatial_gate(u, g, w_s, b_s):
    B, S, _ = g.shape
    causal = jnp.tril(jnp.ones((CHUNK_B, CHUNK_B), dtype=bool))
    w = jnp.where(causal, w_s, 0).astype(g.dtype)
    gb = g.reshape(B, S // CHUNK_B, CHUNK_B, G_B, C_B)
    mixed = jnp.einsum('gts,bnsgc->bntgc', w, gb) + b_s.T.astype(g.dtype)[:, :, None]
    return u * mixed.reshape(B, S, W_B)


def merge_ab(o, gate, subln_g, w_out, layer):
    B, S = o.shape[:2]
    o = (rmsnorm(o, subln_g) * (1.0 - lambda_init(layer))).reshape(B, S, W_QK_A)
    return jnp.concatenate([o, gate.astype(o.dtype)], axis=-1) @ w_out


def mixer_ab_prompt(h, w_in, w_out, lam_p, subln_g, ln_g, ln_b, w_s, b_s, rel_bias, layer):
    B, S, _ = h.shape
    q, k, v, u, g = project_ab(h, w_in, ln_g, ln_b)
    lam = diff_lambda(lam_p, layer)
    k_pos = jnp.arange(S)

    def q_block(i):
        qb = lax.dynamic_slice_in_dim(q, i * Q_BLOCK, Q_BLOCK, axis=1)
        return diff_attend(qb, k, v, i * Q_BLOCK + jnp.arange(Q_BLOCK), k_pos, lam, rel_bias)

    o = lax.map(q_block, jnp.arange(S // Q_BLOCK))
    o = jnp.moveaxis(o, 0, 1).reshape(B, S, H_A, 2 * D_QK)
    out = merge_ab(o, spatial_gate(u, g, w_s, b_s), subln_g, w_out, layer)
    last = S - ((S - 1) // CHUNK_B) * CHUNK_B
    return out, k.reshape(B, S, H_A, 2 * D_QK), v, g[:, S - last:]


def mixer_ab_sample(h, cache_k, cache_v, page_table, e, w_in, w_out, lam_p, subln_g, ln_g, ln_b, w_s, b_s, rel_bias, layer):
    B, T, _ = h.shape
    P = page_table.shape[1] * cache_k.shape[2]
    q, k, v, u, g = project_ab(h, w_in, ln_g, ln_b)
    lam = diff_lambda(lam_p, layer)
    q_pos = P + jnp.arange(T)
    k_pos = jnp.arange(P + T)

    def one(args):
        qi, ki, vi, pt = args
        kp = cache_k[pt, e].reshape(P, H_A, 2, D_QK).astype(ki.dtype)
        vp = cache_v[pt, e].reshape(P, H_A, 2 * D_QK).astype(vi.dtype)
        k_all = jnp.concatenate([kp, ki], axis=0)[None]
        v_all = jnp.concatenate([vp, vi], axis=0)[None]
        return diff_attend(qi[None], k_all, v_all, q_pos, k_pos, lam, rel_bias)[0]

    o = lax.map(one, (q, k, v, page_table))
    t_pad = -(-T // CHUNK_B) * CHUNK_B
    pad = ((0, 0), (0, t_pad - T), (0, 0))
    gate = spatial_gate(jnp.pad(u, pad), jnp.pad(g, pad), w_s, b_s)[:, :T]
    out = merge_ab(o, gate, subln_g, w_out, layer)
    return out, k.reshape(B, T, H_A, 2 * D_QK), v, g


def project_c(h, w_in):
    B, S, _ = h.shape
    q, k, v = jnp.split(h @ w_in, 3, axis=-1)
    shp = (B, S, H_C, D_C)
    return q.reshape(shp), k.reshape(shp), v.reshape(shp)


def branch_bias(rel_bias, dil, nstep):
    return rel_bias[:, :H_C].astype(jnp.float32)[rel_bucket(jnp.arange(nstep + 1) * dil)]


def combine_branches(outs, lses, dtype):
    alpha = jax.nn.softmax(jnp.stack(lses, axis=0), axis=0)
    o = jnp.sum(alpha[..., None] * jnp.stack(outs, axis=0).astype(jnp.float32), axis=0)
    return o.astype(dtype)


def dilated_branch_prompt(q, k, v, dil, nstep, bias):
    B, S, H, E = q.shape
    span = dil * nstep
    s_pad = -(-S // span) * span
    nb = s_pad // span

    def to_blocks(t):
        t = jnp.pad(t, ((0, 0), (0, s_pad - S), (0, 0), (0, 0)))
        t = t.reshape(B, s_pad // dil, dil, H, E).transpose(0, 2, 1, 3, 4)
        return t.reshape(B, dil, nb, nstep, H, E)

    def with_prev(t):
        prev = jnp.pad(t, ((0, 0), (0, 0), (1, 0), (0, 0), (0, 0), (0, 0)))[:, :, :-1]
        return jnp.concatenate([prev, t], axis=3)

    qb = to_blocks(q)
    kk = with_prev(to_blocks(k))
    vv = with_prev(to_blocks(v))
    s = jnp.einsum('bdnqhe,bdnkhe->bdnhqk', qb, kk).astype(jnp.float32) * (E ** -0.5)
    step = nstep + jnp.arange(nstep)[:, None] - jnp.arange(2 * nstep)[None, :]
    band = (step >= 0) & (step <= nstep)
    s = s + jnp.transpose(bias[jnp.clip(step, 0, nstep)], (2, 0, 1))
    no_prev = (jnp.arange(nb)[:, None] == 0) & (jnp.arange(2 * nstep)[None, :] < nstep)
    mask = band[None, None, None, None] & ~no_prev[None, None, :, None, None, :]
    p, lse = softmax_stats(s, mask)
    o = jnp.einsum('bdnhqk,bdnkhe->bdnqhe', p.astype(v.dtype), vv)
    o = o.reshape(B, dil, s_pad // dil, H, E).transpose(0, 2, 1, 3, 4).reshape(B, s_pad, H, E)[:, :S]
    lse = lse.transpose(0, 1, 2, 4, 3).reshape(B, dil, s_pad // dil, H).transpose(0, 2, 1, 3).reshape(B, s_pad, H)[:, :S]
    return o, lse


def mixer_c_prompt(h, w_in, w_out, rel_bias):
    B, S, _ = h.shape
    q, k, v = project_c(h, w_in)
    outs, lses = [], []
    for window, dil in DILATED_BRANCHES:
        nstep = window // dil
        o, lse = dilated_branch_prompt(q, k, v, dil, nstep, branch_bias(rel_bias, dil, nstep))
        outs.append(o)
        lses.append(lse)
    out = combine_branches(outs, lses, h.dtype).reshape(B, S, D_MODEL) @ w_out
    keep = min(MAX_WINDOW, S)
    return out, k[:, S - keep:], v[:, S - keep:]


def mixer_c_sample(h, cache_k, cache_v, o_idx, w_in, w_out, rel_bias):
    B, T, _ = h.shape
    W = cache_k.shape[2]
    q, k, v = project_c(h, w_in)
    biases = [branch_bias(rel_bias, dil, w // dil) for w, dil in DILATED_BRANCHES]

    def one(args):
        qi, ki, vi, b = args
        k_ext = jnp.concatenate([cache_k[o_idx, b].astype(ki.dtype), ki], axis=0)
        v_ext = jnp.concatenate([cache_v[o_idx, b].astype(vi.dtype), vi], axis=0)
        outs, lses = [], []
        for (window, dil), bias in zip(DILATED_BRANCHES, biases):
            nstep = window // dil
            rows = W + jnp.arange(T)[:, None] - dil * jnp.arange(nstep + 1)[None, :]
            valid = rows >= 0
            rows = jnp.maximum(rows, 0)
            kg = k_ext[rows]
            vg = v_ext[rows]
            s = jnp.einsum('qhe,qkhe->hqk', qi, kg).astype(jnp.float32) * (D_C ** -0.5) + bias.T[:, None, :]
            p, lse = softmax_stats(s, valid[None])
            outs.append(jnp.einsum('hqk,qkhe->qhe', p.astype(vi.dtype), vg))
            lses.append(lse.T)
        return combine_branches(outs, lses, qi.dtype)

    o = lax.map(one, (q, k, v, jnp.arange(B)))
    out = o.reshape(B, T, D_MODEL) @ w_out
    return out, k, v


def setup_inputs(seed: int = 0) -> dict:
    key = jax.random.key(seed)
    ks = jax.random.split(key, 24)
    f32 = jnp.float32
    n_pages = PAST_LEN // PAGE_SIZE
    n_used = DEC_BATCH * n_pages
    n_phys = n_used + max(1, n_used // 4)
    w_buf = min(MAX_WINDOW, PAST_LEN)

    def nrm(k, shape, scale):
        return scale * jax.random.normal(k, shape, f32)

    page_table = jax.random.permutation(ks[4], n_phys)[:n_used].reshape(DEC_BATCH, n_pages).astype(jnp.int32)
    return {
        'x_prompt': nrm(ks[0], (BATCH, SEQ, D_MODEL), 1.0),
        'x_sample': nrm(ks[1], (DEC_BATCH, DEC_SEQ, D_MODEL), 1.0),
        'cache_k_a': nrm(ks[2], (n_phys, N_EVEN, PAGE_SIZE, H_A, 2 * D_QK), 1.0),
        'cache_v_a': nrm(ks[3], (n_phys, N_EVEN, PAGE_SIZE, H_A, 2 * D_QK), 1.0),
        'page_table': page_table,
        'cache_k_c': nrm(ks[5], (N_ODD, DEC_BATCH, w_buf, H_C, D_C), 1.0),
        'cache_v_c': nrm(ks[6], (N_ODD, DEC_BATCH, w_buf, H_C, D_C), 1.0),
        'rel_bias': nrm(ks[7], (N_BUCKETS, N_REL_HEADS), 0.5),
        'norm_gains': 1.0 + nrm(ks[8], (DEPTH, 3, 2, D_MODEL), 0.05),
        'w_ffn_gate': nrm(ks[9], (DEPTH, 2, D_MODEL, D_FF), D_MODEL ** -0.5),
        'w_ffn_up': nrm(ks[10], (DEPTH, 2, D_MODEL, D_FF), D_MODEL ** -0.5),
        'w_ffn_down': nrm(ks[11], (DEPTH, 2, D_FF, D_MODEL), D_FF ** -0.5),
        'w_in_ab': nrm(ks[12], (N_EVEN, D_MODEL, W_AB_IN), D_MODEL ** -0.5),
        'w_out_ab': nrm(ks[13], (N_EVEN, W_AB_OUT, D_MODEL), W_AB_OUT ** -0.5),
        'lambda_qk': nrm(ks[14], (N_EVEN, 4, D_QK), 0.1),
        'subln_gain': 1.0 + nrm(ks[15], (N_EVEN, 2 * D_QK), 0.05),
        'ln_v_gain': 1.0 + nrm(ks[16], (N_EVEN, W_B), 0.05),
        'ln_v_bias': nrm(ks[17], (N_EVEN, W_B), 0.02),
        'w_spatial': nrm(ks[18], (N_EVEN, G_B, CHUNK_B, CHUNK_B), CHUNK_B ** -0.5),
        'b_spatial': 1.0 + nrm(ks[19], (N_EVEN, G_B, CHUNK_B), 0.05),
        'w_in_c': nrm(ks[20], (N_ODD, D_MODEL, 3 * D_MODEL), D_MODEL ** -0.5),
        'w_out_c': nrm(ks[21], (N_ODD, D_MODEL, D_MODEL), D_MODEL ** -0.5),
    }


def reference(x_prompt, x_sample, cache_k_a, cache_v_a, page_table, cache_k_c, cache_v_c, rel_bias, norm_gains, w_ffn_gate, w_ffn_up, w_ffn_down, w_in_ab, w_out_ab, lambda_qk, subln_gain, ln_v_gain, ln_v_bias, w_spatial, b_spatial, w_in_c, w_out_c):
    xp, xs = x_prompt, x_sample
    ka_p, va_p, ka_s, va_s, vb_p, vb_s = [], [], [], [], [], []
    kc_p, vc_p, kc_s, vc_s = [], [], [], []
    for li in range(DEPTH):
        ng = norm_gains[li]

        def half_ffn(x, j):
            f = swiglu(rmsnorm(x, ng[2 * j, 0]), w_ffn_gate[li, j], w_ffn_up[li, j], w_ffn_down[li, j])
            return x + 0.5 * rmsnorm(f, ng[2 * j, 1])

        xp, xs = half_ffn(xp, 0), half_ffn(xs, 0)
        hp, hs = rmsnorm(xp, ng[1, 0]), rmsnorm(xs, ng[1, 0])
        if li % 2 == 0:
            e = li // 2
            w = (w_in_ab[e], w_out_ab[e], lambda_qk[e], subln_gain[e], ln_v_gain[e], ln_v_bias[e], w_spatial[e], b_spatial[e], rel_bias, li)
            mp, k_new, v_new, g_new = mixer_ab_prompt(hp, *w)
            ka_p.append(k_new); va_p.append(v_new); vb_p.append(g_new)
            ms, k_new, v_new, g_new = mixer_ab_sample(hs, cache_k_a, cache_v_a, page_table, e, *w)
            ka_s.append(k_new); va_s.append(v_new); vb_s.append(g_new)
        else:
            o = li // 2
            mp, k_new, v_new = mixer_c_prompt(hp, w_in_c[o], w_out_c[o], rel_bias)
            kc_p.append(k_new); vc_p.append(v_new)
            ms, k_new, v_new = mixer_c_sample(hs, cache_k_c, cache_v_c, o, w_in_c[o], w_out_c[o], rel_bias)
            kc_s.append(k_new); vc_s.append(v_new)
        xp = xp + rmsnorm(mp, ng[1, 1])
        xs = xs + rmsnorm(ms, ng[1, 1])
        xp, xs = half_ffn(xp, 1), half_ffn(xs, 1)
    k_a_prompt = jnp.stack(ka_p, axis=1)
    v_a_prompt = jnp.stack(va_p, axis=1)
    k_a_sample = jnp.stack(ka_s, axis=1)
    v_a_sample = jnp.stack(va_s, axis=1)
    vb_prompt = jnp.stack(vb_p, axis=0)
    vb_sample = jnp.stack(vb_s, axis=0)
    k_c_prompt = jnp.stack(kc_p, axis=0)
    v_c_prompt = jnp.stack(vc_p, axis=0)
    k_c_sample = jnp.stack(kc_s, axis=0)
    v_c_sample = jnp.stack(vc_s, axis=0)
    return (xp, xs, k_a_prompt, v_a_prompt, k_a_sample, v_a_sample, vb_prompt, vb_sample, k_c_prompt, v_c_prompt, k_c_sample, v_c_sample)
```

```python
import functools
import math

import numpy as np
import jax
import jax.numpy as jnp
from jax import lax
from jax.experimental import pallas as pl
from jax.experimental.pallas import tpu as pltpu

F32 = jnp.float32
BF16 = jnp.bfloat16

EPS = 1e-6
NEG_INF = -1e30
LANE = 128
VMEM_LIMIT = 56 * 1024 * 1024

H_A = 8
D_QK = 64
G_B = 8
C_B = 128
CHUNK_B = 128
H_C = 16
D_C = 128
DILATED_BRANCHES = ((128, 1), (512, 4), (2048, 16))
N_BUCKETS = 32
MAX_DISTANCE = 128
MASKED = N_BUCKETS


def _params(*sem):
    return pltpu.CompilerParams(dimension_semantics=sem, vmem_limit_bytes=VMEM_LIMIT)


def _rms(x, g):
    return x * lax.rsqrt(jnp.mean(x * x, axis=-1, keepdims=True) + EPS) * g


def _gelu(x):
    return 0.5 * x * (1.0 + lax.erf(x * np.float32(math.sqrt(0.5))))


def _dot(a, b):
    return jnp.dot(a, b, preferred_element_type=F32)


def _dot_nt(a, b):
    return lax.dot_general(a, b, (((1,), (1,)), ((), ())), preferred_element_type=F32)


def _dot_tn(a, b):
    return lax.dot_general(a, b, (((0,), (0,)), ((), ())), preferred_element_type=F32)


def _lambda_init(layer):
    return 0.8 - 0.6 * math.exp(-0.3 * layer)


def _diff_lambda(lam_ref, layer):
    lp = lam_ref[...]
    a = jnp.sum(lp[0:1] * lp[1:2], axis=-1, keepdims=True)
    b = jnp.sum(lp[2:3] * lp[3:4], axis=-1, keepdims=True)
    return jnp.exp(a) - jnp.exp(b) + np.float32(_lambda_init(layer))


def _ffn_kernel(x_ref, gpre_ref, gpost_ref, wg_ref, wu_ref, wd_ref, o_ref, h_ref):
    f = pl.program_id(1)

    @pl.when(f == 0)
    def _():
        h_ref[...] = _rms(x_ref[...], gpre_ref[...]).astype(BF16)
        o_ref[...] = jnp.zeros_like(o_ref)

    h = h_ref[...]
    a = _dot(h, wg_ref[...])
    b = _dot(h, wu_ref[...])
    s = (a * jax.nn.sigmoid(a)) * b
    o_ref[...] += _dot(s.astype(BF16), wd_ref[...])

    @pl.when(f == pl.num_programs(1) - 1)
    def _():
        o_ref[...] = x_ref[...] + 0.5 * _rms(o_ref[...], gpost_ref[...])


def _ffn_half(x, g_pre, g_post, wg, wu, wd, *, tm=512, tf=512):
    n, d = x.shape
    f = wg.shape[1]
    return pl.pallas_call(
        _ffn_kernel,
        grid=(n // tm, f // tf),
        in_specs=[
            pl.BlockSpec((tm, d), lambda i, j: (i, 0)),
            pl.BlockSpec((1, d), lambda i, j: (0, 0)),
            pl.BlockSpec((1, d), lambda i, j: (0, 0)),
            pl.BlockSpec((d, tf), lambda i, j: (0, j)),
            pl.BlockSpec((d, tf), lambda i, j: (0, j)),
            pl.BlockSpec((tf, d), lambda i, j: (j, 0)),
        ],
        out_specs=pl.BlockSpec((tm, d), lambda i, j: (i, 0)),
        out_shape=jax.ShapeDtypeStruct((n, d), F32),
        scratch_shapes=[pltpu.VMEM((tm, d), BF16)],
        compiler_params=_params("parallel", "arbitrary"),
        name="ffn_half",
    )(x, g_pre.reshape(1, d), g_post.reshape(1, d), wg, wu, wd)


def _proj_ab_kernel(x_ref, g_ref, w_ref, lng_ref, lnb_ref, zf_ref, zb_ref, h_ref):
    j = pl.program_id(1)

    @pl.when(j == 0)
    def _():
        h_ref[...] = _rms(x_ref[...], g_ref[...]).astype(BF16)

    z = _dot(h_ref[...], w_ref[...])

    @pl.when(j == 0)
    def _():
        zb_ref[...] = (z * np.float32(D_QK ** -0.5)).astype(BF16)

    @pl.when(jnp.logical_or(j == 1, j == 2))
    def _():
        zf_ref[...] = z
        zb_ref[...] = z.astype(BF16)

    @pl.when(j == 3)
    def _():
        zf_ref[...] = _gelu(z)

    @pl.when(j == 4)
    def _():
        a = _gelu(z)
        c = a - jnp.mean(a, axis=-1, keepdims=True)
        y = c * lax.rsqrt(jnp.mean(c * c, axis=-1, keepdims=True) + EPS)
        zf_ref[...] = y * lng_ref[...] + lnb_ref[...]


def _proj_ab(x, g_pre, w_in, ln_g, ln_b, *, tm=512):
    n, d = x.shape
    w = w_in.shape[1] // 5
    return pl.pallas_call(
        _proj_ab_kernel,
        grid=(n // tm, 5),
        in_specs=[
            pl.BlockSpec((tm, d), lambda i, j: (i, 0)),
            pl.BlockSpec((1, d), lambda i, j: (0, 0)),
            pl.BlockSpec((d, w), lambda i, j: (0, j)),
            pl.BlockSpec((1, w), lambda i, j: (0, 0)),
            pl.BlockSpec((1, w), lambda i, j: (0, 0)),
        ],
        out_specs=[
            pl.BlockSpec((tm, w), lambda i, j: (i, jnp.maximum(j - 1, 0))),
            pl.BlockSpec((tm, w), lambda i, j: (i, jnp.minimum(j, 2))),
        ],
        out_shape=[jax.ShapeDtypeStruct((n, 4 * w), F32), jax.ShapeDtypeStruct((n, 3 * w), BF16)],
        scratch_shapes=[pltpu.VMEM((tm, d), BF16)],
        compiler_params=_params("parallel", "arbitrary"),
        name="proj_ab",
    )(x, g_pre.reshape(1, d), w_in, ln_g.reshape(1, w), ln_b.reshape(1, w))


def _proj_c_kernel(x_ref, g_ref, w_ref, zf_ref, zb_ref, h_ref):
    j = pl.program_id(1)

    @pl.when(j == 0)
    def _():
        h_ref[...] = _rms(x_ref[...], g_ref[...]).astype(BF16)

    z = _dot(h_ref[...], w_ref[...])

    @pl.when(j == 0)
    def _():
        zb_ref[...] = (z * np.float32(D_C ** -0.5)).astype(BF16)

    @pl.when(j > 0)
    def _():
        zf_ref[...] = z
        zb_ref[...] = z.astype(BF16)


def _proj_c(x, g_pre, w_in, *, tm=512):
    n, d = x.shape
    w = w_in.shape[1] // 3
    return pl.pallas_call(
        _proj_c_kernel,
        grid=(n // tm, 3),
        in_specs=[
            pl.BlockSpec((tm, d), lambda i, j: (i, 0)),
            pl.BlockSpec((1, d), lambda i, j: (0, 0)),
            pl.BlockSpec((d, w), lambda i, j: (0, j)),
        ],
        out_specs=[
            pl.BlockSpec((tm, w), lambda i, j: (i, jnp.maximum(j - 1, 0))),
            pl.BlockSpec((tm, w), lambda i, j: (i, j)),
        ],
        out_shape=[jax.ShapeDtypeStruct((n, 2 * w), F32), jax.ShapeDtypeStruct((n, 3 * w), BF16)],
        scratch_shapes=[pltpu.VMEM((tm, d), BF16)],
        compiler_params=_params("parallel", "arbitrary"),
        name="proj_c",
    )(x, g_pre.reshape(1, d), w_in)


def _out_proj_kernel(a_ref, b_ref, wa_ref, wb_ref, x_ref, g_ref, o_ref):
    y = _dot(a_ref[...].astype(BF16), wa_ref[...]) + _dot(b_ref[...].astype(BF16), wb_ref[...])
    o_ref[...] = x_ref[...] + _rms(y, g_ref[...])


def _out_proj(a, b, w_out, x, g_post, *, a_blk=0, b_blk=0, tm=512):
    n, d = x.shape
    kh = w_out.shape[0] // 2
    return pl.pallas_call(
        _out_proj_kernel,
        grid=(n // tm,),
        in_specs=[
            pl.BlockSpec((tm, kh), lambda i: (i, a_blk)),
            pl.BlockSpec((tm, kh), lambda i: (i, b_blk)),
            pl.BlockSpec((kh, d), lambda i: (0, 0)),
            pl.BlockSpec((kh, d), lambda i: (1, 0)),
            pl.BlockSpec((tm, d), lambda i: (i, 0)),
            pl.BlockSpec((1, d), lambda i: (0, 0)),
        ],
        out_specs=pl.BlockSpec((tm, d), lambda i: (i, 0)),
        out_shape=jax.ShapeDtypeStruct((n, d), F32),
        compiler_params=_params("parallel"),
        name="out_proj",
    )(a, b, w_out, w_out, x, g_post.reshape(1, d))


def _bucket_np(dist):
    max_exact = N_BUCKETS // 2
    d = np.maximum(dist, 0)
    ratio = np.log(np.maximum(d, 1).astype(np.float32) / np.float32(max_exact)) / np.float32(math.log(MAX_DISTANCE / max_exact))
    large = np.minimum(max_exact + (ratio * (N_BUCKETS - max_exact)).astype(np.int32), N_BUCKETS - 1)
    return np.where(d < max_exact, d, large).astype(np.int32)


def _bias_expand_kernel(idx_ref, tab_ref, o_ref):
    idx = idx_ref[...]
    val = jnp.zeros(idx.shape, F32)
    for b in range(N_BUCKETS + 1):
        val = jnp.where(idx == b, tab_ref[b:b + 1, :], val)
    o_ref[...] = val


def _bias_expand(idx, tab, *, tr=256):
    gi, r, c = idx.shape
    g = tab.shape[0]
    tr = min(tr, r)
    return pl.pallas_call(
        _bias_expand_kernel,
        grid=(g, r // tr),
        in_specs=[
            pl.BlockSpec((None, tr, c), (lambda a, i: (a, i, 0)) if gi == g else (lambda a, i: (0, i, 0))),
            pl.BlockSpec((None, N_BUCKETS + 1, c), lambda a, i: (a, 0, 0)),
        ],
        out_specs=pl.BlockSpec((None, tr, c), lambda a, i: (a, i, 0)),
        out_shape=jax.ShapeDtypeStruct((g, r, c), F32),
        compiler_params=_params("parallel", "parallel"),
        name="bias_expand",
    )(jnp.asarray(idx), tab)


def _head_table(rel_bias, heads_of_col):
    t = rel_bias.astype(F32)[:, np.asarray(heads_of_col)]
    return jnp.concatenate([t, jnp.full((1, t.shape[1]), NEG_INF, F32)], axis=0)


def _attn_a_prompt_kernel(q_ref, k_ref, v_ref, bias_ref, lam_ref, sg_ref, o_ref, m_ref, l_ref, acc_ref, *, tq, layer):
    qi = pl.program_id(1)
    q = q_ref[...]
    lane = lax.broadcasted_iota(jnp.int32, q.shape, 1)
    zero = jnp.zeros_like(q)
    qs = (jnp.where(lane < D_QK, q, zero), jnp.where(lane >= D_QK, q, zero))
    m_ref[...] = jnp.full(m_ref.shape, NEG_INF, F32)
    l_ref[...] = jnp.zeros(l_ref.shape, F32)
    acc_ref[...] = jnp.zeros(acc_ref.shape, F32)

    def body(j, carry):
        kind = jnp.minimum(qi - j, 2)
        bias = bias_ref[kind]
        off = pl.multiple_of(j * tq, tq)
        k = k_ref[pl.ds(off, tq), :]
        v = v_ref[pl.ds(off, tq), :]
        for i in range(2):
            s = _dot_nt(qs[i], k) + bias
            m_prev = m_ref[i]
            m_new = jnp.maximum(m_prev, jnp.max(s, axis=-1, keepdims=True))
            alpha = jnp.exp(m_prev - m_new)
            p = jnp.exp(s - m_new)
            l_ref[i] = alpha * l_ref[i] + jnp.sum(p, axis=-1, keepdims=True)
            acc_ref[i] = alpha * acc_ref[i] + _dot(p.astype(BF16), v)
            m_ref[i] = m_new
        return carry

    lax.fori_loop(0, qi + 1, body, 0)
    lam = _diff_lambda(lam_ref, layer)
    o = acc_ref[0] / l_ref[0] - lam * (acc_ref[1] / l_ref[1])
    o_ref[...] = (_rms(o, sg_ref[...]) * np.float32(1.0 - _lambda_init(layer))).astype(o_ref.dtype)


def _attn_a_prompt(zb, s_len, bias_tiles, lam_p, subln_g, layer, *, tq=512):
    w = zb.shape[1] // 3
    hw = w // H_A
    return pl.pallas_call(
        functools.partial(_attn_a_prompt_kernel, tq=tq, layer=layer),
        grid=(H_A, s_len // tq),
        in_specs=[
            pl.BlockSpec((tq, hw), lambda h, i: (i, h)),
            pl.BlockSpec((s_len, hw), lambda h, i: (0, H_A + h)),
            pl.BlockSpec((s_len, hw), lambda h, i: (0, 2 * H_A + h)),
            pl.BlockSpec((None, 3, tq, tq), lambda h, i: (h, 0, 0, 0)),
            pl.BlockSpec((4, D_QK), lambda h, i: (0, 0)),
            pl.BlockSpec((1, hw), lambda h, i: (0, 0)),
        ],
        out_specs=pl.BlockSpec((tq, hw), lambda h, i: (i, h)),
        out_shape=jax.ShapeDtypeStruct((s_len, w), BF16),
        scratch_shapes=[pltpu.VMEM((2, tq, 1), F32), pltpu.VMEM((2, tq, 1), F32), pltpu.VMEM((2, tq, hw), F32)],
        compiler_params=_params("parallel", "arbitrary"),
        name="attn_a_prompt",
    )(zb, zb, zb, bias_tiles, lam_p, subln_g.reshape(1, hw))


def _bias_tiles_a_prompt(rel_bias, tq):
    i = np.arange(tq)[:, None]
    j = np.arange(tq)[None, :]
    idx = []
    for delta in (0, tq, 2 * tq):
        dist = delta + i - j
        idx.append(np.where(dist >= 0, _bucket_np(dist), MASKED))
    idx = np.stack(idx).reshape(1, 3 * tq, tq).astype(np.int32)
    tab = jnp.stack([_head_table(rel_bias, np.full(tq, h)) for h in range(H_A)])
    return _bias_expand(idx, tab, tr=min(tq, 256)).reshape(H_A, 3, tq, tq)


def _attn_a_sample_kernel(pt_ref, wq_ref, kn_ref, vn_ref, bp_ref, bn_ref, lam_ref, sg_ref, *rest, n_pages, page, layer):
    k_pages = rest[:n_pages]
    v_pages = rest[n_pages:2 * n_pages]
    o_ref = rest[2 * n_pages]
    s_ref, p_ref = rest[2 * n_pages + 1:]
    wq = wq_ref[...]

    def heads_to_lanes(ref):
        return jnp.concatenate([ref[:, h, :] for h in range(H_A)], axis=1).astype(BF16)

    for p in range(n_pages):
        s_ref[p * page:(p + 1) * page, :] = _dot(heads_to_lanes(k_pages[p]), wq) + bp_ref[p * page:(p + 1) * page, :]
    s_new = _dot(kn_ref[...], wq) + bn_ref[...]
    s_past = s_ref[...]
    m = jnp.maximum(jnp.max(s_past, axis=0, keepdims=True), jnp.max(s_new, axis=0, keepdims=True))
    e_past = jnp.exp(s_past - m)
    e_new = jnp.exp(s_new - m)
    den = jnp.sum(e_past, axis=0, keepdims=True) + jnp.sum(e_new, axis=0, keepdims=True)
    col = lax.broadcasted_iota(jnp.int32, den.shape, 1)
    t_new = kn_ref.shape[0]
    lam = _diff_lambda(lam_ref, layer)
    scale = jnp.where((col // t_new) % 2 == 0, 1.0, -lam) / den
    p_ref[...] = (e_past * scale).astype(BF16)
    acc = _dot_tn((e_new * scale).astype(BF16), vn_ref[...])
    for p in range(n_pages):
        acc = acc + _dot_tn(p_ref[p * page:(p + 1) * page, :], heads_to_lanes(v_pages[p]))
    for h in range(H_A):
        r0 = 2 * h * t_new
        o = acc[r0:r0 + t_new, h * LANE:(h + 1) * LANE] + acc[r0 + t_new:r0 + 2 * t_new, h * LANE:(h + 1) * LANE]
        o_ref[:, h * LANE:(h + 1) * LANE] = (_rms(o, sg_ref[...]) * np.float32(1.0 - _lambda_init(layer))).astype(o_ref.dtype)


def _attn_a_sample(zb_s, cache_k, cache_v, page_table, e, bias_past, bias_new, lam_p, subln_g, layer):
    bsz, t, w3 = zb_s.shape
    w = w3 // 3
    n_pages = page_table.shape[1]
    page = cache_k.shape[2]
    assert 2 * H_A * t == LANE
    qt = jnp.swapaxes(zb_s[:, :, :w], 1, 2)
    rows = np.arange(w)[:, None] // D_QK
    cols = np.arange(LANE)[None, :] // t
    wq = jnp.where(jnp.asarray(rows == cols), jnp.tile(qt, (1, 1, 2 * H_A)), jnp.zeros((), BF16))
    kn = zb_s[:, :, w:2 * w]
    vn = zb_s[:, :, 2 * w:]

    def page_spec(p):
        return pl.BlockSpec((None, None, page, H_A, LANE), lambda b, pt: (pt[b * n_pages + p], e, 0, 0, 0))

    grid_spec = pltpu.PrefetchScalarGridSpec(
        num_scalar_prefetch=1,
        grid=(bsz,),
        in_specs=[
            pl.BlockSpec((None, w, LANE), lambda b, pt: (b, 0, 0)),
            pl.BlockSpec((None, t, w), lambda b, pt: (b, 0, 0)),
            pl.BlockSpec((None, t, w), lambda b, pt: (b, 0, 0)),
            pl.BlockSpec((n_pages * page, LANE), lambda b, pt: (0, 0)),
            pl.BlockSpec((t, LANE), lambda b, pt: (0, 0)),
            pl.BlockSpec((4, D_QK), lambda b, pt: (0, 0)),
            pl.BlockSpec((1, LANE), lambda b, pt: (0, 0)),
        ] + [page_spec(p) for p in range(n_pages)] + [page_spec(p) for p in range(n_pages)],
        out_specs=pl.BlockSpec((None, t, w), lambda b, pt: (b, 0, 0)),
        scratch_shapes=[pltpu.VMEM((n_pages * page, LANE), F32), pltpu.VMEM((n_pages * page, LANE), BF16)],
    )
    return pl.pallas_call(
        functools.partial(_attn_a_sample_kernel, n_pages=n_pages, page=page, layer=layer),
        grid_spec=grid_spec,
        out_shape=jax.ShapeDtypeStruct((bsz, t, w), BF16),
        compiler_params=_params("arbitrary"),
        name="attn_a_sample",
    )(page_table.reshape(-1), wq, kn, vn, bias_past, bias_new, lam_p, subln_g.reshape(1, LANE),
      *([cache_k] * n_pages), *([cache_v] * n_pages))


def _bias_a_sample(rel_bias, past, t):
    kpos = np.arange(past + t)[:, None]
    col = np.arange(LANE)[None, :]
    dist = past + (col % t) - kpos
    idx = np.where(dist >= 0, _bucket_np(dist), MASKED).astype(np.int32)
    pad = (-idx.shape[0]) % 8
    idx = np.pad(idx, ((0, pad), (0, 0)), constant_values=MASKED)[None]
    tab = _head_table(rel_bias, np.arange(LANE) // (2 * t))[None]
    out = _bias_expand(idx, tab, tr=idx.shape[1])[0]
    return out[:past], out[past:past + t]


def _gate_prompt_kernel(u_ref, g_ref, w_ref, b_ref, o_ref, *, chunks):
    row = lax.broadcasted_iota(jnp.int32, (CHUNK_B, CHUNK_B), 0)
    col = lax.broadcasted_iota(jnp.int32, (CHUNK_B, CHUNK_B), 1)
    for gi in range(G_B):
        wg = jnp.where(row >= col, w_ref[gi], 0.0).astype(BF16)
        bg = b_ref[:, gi:gi + 1]
        for c in range(chunks):
            rs = slice(c * CHUNK_B, (c + 1) * CHUNK_B)
            cs = slice(gi * C_B, (gi + 1) * C_B)
            mixed = _dot(wg, g_ref[rs, cs].astype(BF16)) + bg
            o_ref[rs, cs] = (u_ref[rs, cs] * mixed).astype(o_ref.dtype)


def _gate_prompt(zf, s_len, w_s, b_s, *, chunks=4):
    w = zf.shape[1] // 4
    tm = chunks * CHUNK_B
    return pl.pallas_call(
        functools.partial(_gate_prompt_kernel, chunks=chunks),
        grid=(s_len // tm,),
        in_specs=[
            pl.BlockSpec((tm, w), lambda i: (i, 2)),
            pl.BlockSpec((tm, w), lambda i: (i, 3)),
            pl.BlockSpec((G_B, CHUNK_B, CHUNK_B), lambda i: (0, 0, 0)),
            pl.BlockSpec((CHUNK_B, G_B), lambda i: (0, 0)),
        ],
        out_specs=pl.BlockSpec((tm, w), lambda i: (i, 0)),
        out_shape=jax.ShapeDtypeStruct((s_len, w), BF16),
        compiler_params=_params("parallel"),
        name="gate_prompt",
    )(zf, zf, w_s, b_s.T)


def _gate_sample_kernel(u_ref, g_ref, w_ref, b_ref, o_ref):
    t = u_ref.shape[1]
    g = g_ref[...]
    mixed = jnp.zeros(g.shape, F32) + b_ref[...][None]
    for s in range(t):
        mixed = mixed + w_ref[:, s, :][None] * g[:, s:s + 1, :]
    o_ref[...] = (u_ref[...] * mixed).astype(o_ref.dtype)


def _gate_sample(u, g, w_s, b_s, *, tb=32):
    bsz, t, w = u.shape
    causal = np.tril(np.ones((t, t), bool))
    wt = jnp.where(jnp.asarray(causal)[None], w_s[:, :t, :t], 0.0)
    wt = jnp.repeat(jnp.transpose(wt, (1, 2, 0)), C_B, axis=2)
    bt = jnp.repeat(b_s[:, :t].T, C_B, axis=1)
    return pl.pallas_call(
        _gate_sample_kernel,
        grid=(bsz // tb,),
        in_specs=[
            pl.BlockSpec((tb, t, w), lambda i: (i, 0, 0)),
            pl.BlockSpec((tb, t, w), lambda i: (i, 0, 0)),
            pl.BlockSpec((t, t, w), lambda i: (0, 0, 0)),
            pl.BlockSpec((t, w), lambda i: (0, 0)),
        ],
        out_specs=pl.BlockSpec((tb, t, w), lambda i: (i, 0, 0)),
        out_shape=jax.ShapeDtypeStruct((bsz, t, w), BF16),
        compiler_params=_params("parallel"),
        name="gate_sample",
    )(u, g, wt, bt)


def _attn_c_prompt_kernel(q_ref, kp_ref, kc_ref, vp_ref, vc_ref, bias_ref, o_ref, lse_ref, *, nstep):
    n = pl.program_id(1)
    no_prev = jnp.where(n == 0, np.float32(NEG_INF), np.float32(0.0))
    for h in range(H_C):
        hs = slice(h * D_C, (h + 1) * D_C)
        q = q_ref[:, hs]
        s_p = _dot_nt(q, kp_ref[:, hs]) + bias_ref[h, :, :nstep] + no_prev
        s_c = _dot_nt(q, kc_ref[:, hs]) + bias_ref[h, :, nstep:]
        m = jnp.maximum(jnp.max(s_p, axis=-1, keepdims=True), jnp.max(s_c, axis=-1, keepdims=True))
        e_p = jnp.exp(s_p - m)
        e_c = jnp.exp(s_c - m)
        den = jnp.sum(e_p, axis=-1, keepdims=True) + jnp.sum(e_c, axis=-1, keepdims=True)
        o = _dot(e_p.astype(BF16), vp_ref[:, hs]) + _dot(e_c.astype(BF16), vc_ref[:, hs])
        o_ref[:, hs] = o / den
        lse_ref[:, hs] = jnp.broadcast_to(m + jnp.log(den), (nstep, D_C))


def _attn_c_prompt_branch(zb, s_len, dil, nstep, bias):
    n_tok, d3 = zb.shape
    d = d3 // 3
    assert n_tok % dil == 0 and s_len % (dil * nstep) == 0
    z = zb.reshape(n_tok // dil, dil * d3)
    nb = s_len // (dil * nstep)
    prev = lambda part: (lambda r, n: (jnp.maximum(n - 1, 0), 3 * r + part))
    cur = lambda part: (lambda r, n: (n, 3 * r + part))
    out, lse = pl.pallas_call(
        functools.partial(_attn_c_prompt_kernel, nstep=nstep),
        grid=(dil, nb),
        in_specs=[
            pl.BlockSpec((nstep, d), cur(0)),
            pl.BlockSpec((nstep, d), prev(1)),
            pl.BlockSpec((nstep, d), cur(1)),
            pl.BlockSpec((nstep, d), prev(2)),
            pl.BlockSpec((nstep, d), cur(2)),
            pl.BlockSpec((H_C, nstep, 2 * nstep), lambda r, n: (0, 0, 0)),
        ],
        out_specs=[pl.BlockSpec((nstep, d), lambda r, n: (n, r))] * 2,
        out_shape=[jax.ShapeDtypeStruct((s_len // dil, dil * d), F32)] * 2,
        compiler_params=_params("parallel", "arbitrary"),
        name=f"attn_c_prompt_d{dil}",
    )(z, z, z, z, z, bias)
    return out.reshape(s_len, d), lse.reshape(s_len, d)


def _bias_tiles_c_prompt(rel_bias):
    idx = []
    for window, dil in DILATED_BRANCHES:
        nstep = window // dil
        step = nstep + np.arange(nstep)[:, None] - np.arange(2 * nstep)[None, :]
        band = (step >= 0) & (step <= nstep)
        idx.append(np.where(band, _bucket_np(np.clip(step, 0, nstep) * dil), MASKED))
    nstep = idx[0].shape[0]
    assert all(i.shape == (nstep, 2 * nstep) for i in idx)
    nbr = len(idx)
    idx = np.stack(idx).reshape(1, nbr * nstep, 2 * nstep).astype(np.int32)
    tab = jnp.stack([_head_table(rel_bias, np.full(2 * nstep, h)) for h in range(H_C)])
    out = _bias_expand(idx, tab, tr=nstep)
    return jnp.swapaxes(out.reshape(H_C, nbr, nstep, 2 * nstep), 0, 1)


def _combine_kernel(o0, o1, o2, l0, l1, l2, out_ref):
    a0, a1, a2 = l0[...], l1[...], l2[...]
    m = jnp.maximum(jnp.maximum(a0, a1), a2)
    e0, e1, e2 = jnp.exp(a0 - m), jnp.exp(a1 - m), jnp.exp(a2 - m)
    den = e0 + e1 + e2
    out_ref[...] = ((e0 / den) * o0[...] + (e1 / den) * o1[...] + (e2 / den) * o2[...]).astype(out_ref.dtype)


def _combine_branches(outs, lses, *, tm=512):
    s_len, d = outs[0].shape
    spec = pl.BlockSpec((tm, d), lambda i: (i, 0))
    return pl.pallas_call(
        _combine_kernel,
        grid=(s_len // tm,),
        in_specs=[spec] * 6,
        out_specs=spec,
        out_shape=jax.ShapeDtypeStruct((s_len, d), BF16),
        compiler_params=_params("parallel"),
        name="combine_c",
    )(*outs, *lses)


def _attn_c_sample_kernel(wq_ref, kn_ref, vn_ref, bp_ref, bn_ref, kc_ref, vc_ref, o_ref, s_ref, p_ref, pn_ref, acc_ref, *, nck, ck):
    c = pl.program_id(1)
    t_new = kn_ref.shape[0]

    def heads_to_lanes(ref):
        return jnp.concatenate([ref[:, h, :] for h in range(H_C)], axis=1).astype(BF16)

    @pl.when(c < nck)
    def _():
        off = pl.multiple_of(c * ck, ck)
        s_ref[pl.ds(off, ck), :] = _dot(heads_to_lanes(kc_ref), wq_ref[...])

    @pl.when(c == nck - 1)
    def _():
        s_past = s_ref[...]
        s_new = _dot(kn_ref[...], wq_ref[...])
        es, lses = [], []
        for b in range(bp_ref.shape[0]):
            sp = s_past + bp_ref[b]
            sn = s_new + bn_ref[b]
            m = jnp.maximum(jnp.max(sp, axis=0, keepdims=True), jnp.max(sn, axis=0, keepdims=True))
            ep = jnp.exp(sp - m)
            en = jnp.exp(sn - m)
            den = jnp.sum(ep, axis=0, keepdims=True) + jnp.sum(en, axis=0, keepdims=True)
            es.append((ep, en, den))
            lses.append(m + jnp.log(den))
        mm = functools.reduce(jnp.maximum, lses)
        ws = [jnp.exp(l - mm) for l in lses]
        wsum = functools.reduce(lambda a, b: a + b, ws)
        pp = jnp.zeros(s_past.shape, F32)
        pn = jnp.zeros(s_new.shape, F32)
        for (ep, en, den), w in zip(es, ws):
            coef = (w / wsum) / den
            pp = pp + coef * ep
            pn = pn + coef * en
        p_ref[...] = pp.astype(BF16)
        pn_ref[...] = pn.astype(BF16)

    @pl.when(c == nck)
    def _():
        acc_ref[...] = _dot_tn(pn_ref[...], vn_ref[...])

    @pl.when(c >= nck)
    def _():
        off = pl.multiple_of((c - nck) * ck, ck)
        acc_ref[...] += _dot_tn(p_ref[pl.ds(off, ck), :], heads_to_lanes(vc_ref))

    @pl.when(c == 2 * nck - 1)
    def _():
        for h in range(H_C):
            o_ref[:, h * D_C:(h + 1) * D_C] = acc_ref[h * t_new:(h + 1) * t_new, h * D_C:(h + 1) * D_C].astype(o_ref.dtype)


def _attn_c_sample(zb_s, cache_k, cache_v, o_idx, bias_past, bias_new, *, ck=512):
    bsz, t, d3 = zb_s.shape
    d = d3 // 3
    wlen = cache_k.shape[2]
    nck = wlen // ck
    nbr = bias_past.shape[0]
    assert H_C * t == LANE and wlen % ck == 0
    qt = jnp.swapaxes(zb_s[:, :, :d], 1, 2)
    rows = np.arange(d)[:, None] // D_C
    cols = np.arange(LANE)[None, :] // t
    wq = jnp.where(jnp.asarray(rows == cols), jnp.tile(qt, (1, 1, H_C)), jnp.zeros((), BF16))
    kn = zb_s[:, :, d:2 * d]
    vn = zb_s[:, :, 2 * d:]
    return pl.pallas_call(
        functools.partial(_attn_c_sample_kernel, nck=nck, ck=ck),
        grid=(bsz, 2 * nck),
        in_specs=[
            pl.BlockSpec((None, d, LANE), lambda b, c: (b, 0, 0)),
            pl.BlockSpec((None, t, d), lambda b, c: (b, 0, 0)),
            pl.BlockSpec((None, t, d), lambda b, c: (b, 0, 0)),
            pl.BlockSpec((nbr, wlen, LANE), lambda b, c: (0, 0, 0)),
            pl.BlockSpec((nbr, t, LANE), lambda b, c: (0, 0, 0)),
            pl.BlockSpec((None, None, ck, H_C, D_C), lambda b, c: (o_idx, b, jnp.minimum(c, nck - 1), 0, 0)),
            pl.BlockSpec((None, None, ck, H_C, D_C), lambda b, c: (o_idx, b, jnp.maximum(c - nck, 0), 0, 0)),
        ],
        out_specs=pl.BlockSpec((None, t, d), lambda b, c: (b, 0, 0)),
        out_shape=jax.ShapeDtypeStruct((bsz, t, d), BF16),
        scratch_shapes=[pltpu.VMEM((wlen, LANE), F32), pltpu.VMEM((wlen, LANE), BF16),
                        pltpu.VMEM((t, LANE), BF16), pltpu.VMEM((LANE, d), F32)],
        compiler_params=_params("parallel", "arbitrary"),
        name="attn_c_sample",
    )(wq, kn, vn, bias_past, bias_new, cache_k, cache_v)


def _bias_c_sample(rel_bias, wlen, t):
    krow = np.arange(wlen + t)[:, None]
    col = np.arange(LANE)[None, :]
    dist = wlen + (col % t) - krow
    idx = []
    for window, dil in DILATED_BRANCHES:
        ok = (dist >= 0) & (dist % dil == 0) & (dist <= window)
        idx.append(np.where(ok, _bucket_np(dist), MASKED))
    idx = np.stack(idx).astype(np.int32)
    pad = (-idx.shape[1]) % 8
    idx = np.pad(idx, ((0, 0), (0, pad), (0, 0)), constant_values=MASKED)
    tab = jnp.broadcast_to(_head_table(rel_bias, np.arange(LANE) // t)[None], (idx.shape[0], N_BUCKETS + 1, LANE))
    out = _bias_expand(idx, tab, tr=idx.shape[1])
    return out[:, :wlen], out[:, wlen:wlen + t]


def kernel(x_prompt, x_sample, cache_k_a, cache_v_a, page_table, cache_k_c, cache_v_c, rel_bias, norm_gains, w_ffn_gate, w_ffn_up, w_ffn_down, w_in_ab, w_out_ab, lambda_qk, subln_gain, ln_v_gain, ln_v_bias, w_spatial, b_spatial, w_in_c, w_out_c):
    bp, s_len, d = x_prompt.shape
    bs, t_new, _ = x_sample.shape
    assert bp == 1
    n_s = bs * t_new
    depth = norm_gains.shape[0]
    x = jnp.concatenate([x_prompt.reshape(s_len, d), x_sample.reshape(n_s, d)], axis=0)

    tq_a = 512
    past_a = page_table.shape[1] * cache_k_a.shape[2]
    bias_a_prompt = _bias_tiles_a_prompt(rel_bias, tq_a)
    bias_a_past, bias_a_new = _bias_a_sample(rel_bias, past_a, t_new)
    bias_c_prompt = _bias_tiles_c_prompt(rel_bias)
    bias_c_past, bias_c_new = _bias_c_sample(rel_bias, cache_k_c.shape[2], t_new)

    ka_p, va_p, ka_s, va_s, vb_p, vb_s = [], [], [], [], [], []
    kc_p, vc_p, kc_s, vc_s = [], [], [], []
    for li in range(depth):
        ng = norm_gains[li]

        def half_ffn(x, j):
            return _ffn_half(x, ng[2 * j, 0], ng[2 * j, 1], w_ffn_gate[li, j].astype(BF16),
                             w_ffn_up[li, j].astype(BF16), w_ffn_down[li, j].astype(BF16))

        x = half_ffn(x, 0)
        if li % 2 == 0:
            e = li // 2
            w = H_A * 2 * D_QK
            zf, zb = _proj_ab(x, ng[1, 0], w_in_ab[e].astype(BF16), ln_v_gain[e], ln_v_bias[e])
            ka_p.append(zf[:s_len, :w].reshape(1, s_len, H_A, 2 * D_QK))
            va_p.append(zf[:s_len, w:2 * w].reshape(1, s_len, H_A, 2 * D_QK))
            ka_s.append(zf[s_len:, :w].reshape(bs, t_new, H_A, 2 * D_QK))
            va_s.append(zf[s_len:, w:2 * w].reshape(bs, t_new, H_A, 2 * D_QK))
            last = s_len - ((s_len - 1) // CHUNK_B) * CHUNK_B
            vb_p.append(zf[s_len - last:s_len, 3 * w:].reshape(1, last, w))
            vb_s.append(zf[s_len:, 3 * w:].reshape(bs, t_new, w))

            o_p = _attn_a_prompt(zb, s_len, bias_a_prompt, lambda_qk[e], subln_gain[e], li, tq=tq_a)
            o_s = _attn_a_sample(zb[s_len:].reshape(bs, t_new, 3 * w), cache_k_a, cache_v_a, page_table, e,
                                 bias_a_past, bias_a_new, lambda_qk[e], subln_gain[e], li)
            g_p = _gate_prompt(zf, s_len, w_spatial[e], b_spatial[e])
            g_s = _gate_sample(zf[s_len:, 2 * w:3 * w].reshape(bs, t_new, w), zf[s_len:, 3 * w:].reshape(bs, t_new, w),
                               w_spatial[e], b_spatial[e])
            a = jnp.concatenate([o_p, o_s.reshape(n_s, w)], axis=0)
            b = jnp.concatenate([g_p, g_s.reshape(n_s, w)], axis=0)
            x = _out_proj(a, b, w_out_ab[e].astype(BF16), x, ng[1, 1])
        else:
            o = li // 2
            zf, zb = _proj_c(x, ng[1, 0], w_in_c[o].astype(BF16))
            keep = min(max(wd for wd, _ in DILATED_BRANCHES), s_len)
            kc_p.append(zf[s_len - keep:s_len, :d].reshape(1, keep, H_C, D_C))
            vc_p.append(zf[s_len - keep:s_len, d:].reshape(1, keep, H_C, D_C))
            kc_s.append(zf[s_len:, :d].reshape(bs, t_new, H_C, D_C))
            vc_s.append(zf[s_len:, d:].reshape(bs, t_new, H_C, D_C))
            outs, lses = [], []
            for bi, (window, dil) in enumerate(DILATED_BRANCHES):
                ob, lb = _attn_c_prompt_branch(zb, s_len, dil, window // dil, bias_c_prompt[bi])
                outs.append(ob)
                lses.append(lb)
            o_p = _combine_branches(outs, lses)
            o_s = _attn_c_sample(zb[s_len:].reshape(bs, t_new, 3 * d), cache_k_c, cache_v_c, o, bias_c_past, bias_c_new)
            ab = jnp.concatenate([o_p, o_s.reshape(n_s, d)], axis=0)
            x = _out_proj(ab, ab, w_out_c[o].astype(BF16), x, ng[1, 1], a_blk=0, b_blk=1)
        x = half_ffn(x, 1)

    y_prompt = x[:s_len].reshape(1, s_len, d)
    y_sample = x[s_len:].reshape(bs, t_new, d)
    return (y_prompt, y_sample,
            jnp.stack(ka_p, axis=1), jnp.stack(va_p, axis=1), jnp.stack(ka_s, axis=1), jnp.stack(va_s, axis=1),
            jnp.stack(vb_p, axis=0), jnp.stack(vb_s, axis=0),
            jnp.stack(kc_p, axis=0), jnp.stack(vc_p, axis=0), jnp.stack(kc_s, axis=0), jnp.stack(vc_s, axis=0))
```

```python
import functools
import math

import numpy as np
import jax
import jax.numpy as jnp
from jax import lax
from jax.experimental import pallas as pl
from jax.experimental.pallas import tpu as pltpu

F32 = jnp.float32
BF16 = jnp.bfloat16

EPS = 1e-6
NEG_INF = -1e30
LOG2E = math.log2(math.e)
LANE = 128
VMEM_LIMIT = 56 * 1024 * 1024

H_A = 8
D_QK = 64
G_B = 8
C_B = 128
CHUNK_B = 128
H_C = 16
D_C = 128
DILATED_BRANCHES = ((128, 1), (512, 4), (2048, 16))
N_BUCKETS = 32
MAX_DISTANCE = 128
MASKED = N_BUCKETS


def _params(*sem):
    return pltpu.CompilerParams(dimension_semantics=sem, vmem_limit_bytes=VMEM_LIMIT)


def _rms(x, g):
    return x * lax.rsqrt(jnp.mean(x * x, axis=-1, keepdims=True) + EPS) * g


def _gelu(x):
    return 0.5 * x * (1.0 + lax.erf(x * np.float32(math.sqrt(0.5))))


def _dot(a, b):
    return jnp.dot(a, b, preferred_element_type=F32)


def _dot_nt(a, b):
    return lax.dot_general(a, b, (((1,), (1,)), ((), ())), preferred_element_type=F32)


def _dot_tn(a, b):
    return lax.dot_general(a, b, (((0,), (0,)), ((), ())), preferred_element_type=F32)


def _lambda_init(layer):
    return 0.8 - 0.6 * math.exp(-0.3 * layer)


def _diff_lambda(lam_ref, layer):
    lp = lam_ref[...]
    a = jnp.sum(lp[0:1] * lp[1:2], axis=-1, keepdims=True)
    b = jnp.sum(lp[2:3] * lp[3:4], axis=-1, keepdims=True)
    return jnp.exp(a) - jnp.exp(b) + np.float32(_lambda_init(layer))


def _ffn_kernel(x_ref, gpre_ref, gpost_ref, wg_ref, wu_ref, wd_ref, o_ref, h_ref):
    f = pl.program_id(1)

    @pl.when(f == 0)
    def _():
        h_ref[...] = _rms(x_ref[...], gpre_ref[...]).astype(BF16)
        o_ref[...] = jnp.zeros_like(o_ref)

    h = h_ref[...]
    a = _dot(h, wg_ref[...])
    b = _dot(h, wu_ref[...])
    s = (a * jax.nn.sigmoid(a)) * b
    o_ref[...] += _dot(s.astype(BF16), wd_ref[...])

    @pl.when(f == pl.num_programs(1) - 1)
    def _():
        o_ref[...] = x_ref[...] + 0.5 * _rms(o_ref[...], gpost_ref[...])


def _ffn_half(x, g_pre, g_post, wg, wu, wd, *, tm=512, tf=512):
    n, d = x.shape
    f = wg.shape[1]
    return pl.pallas_call(
        _ffn_kernel,
        grid=(n // tm, f // tf),
        in_specs=[
            pl.BlockSpec((tm, d), lambda i, j: (i, 0)),
            pl.BlockSpec((1, d), lambda i, j: (0, 0)),
            pl.BlockSpec((1, d), lambda i, j: (0, 0)),
            pl.BlockSpec((d, tf), lambda i, j: (0, j)),
            pl.BlockSpec((d, tf), lambda i, j: (0, j)),
            pl.BlockSpec((tf, d), lambda i, j: (j, 0)),
        ],
        out_specs=pl.BlockSpec((tm, d), lambda i, j: (i, 0)),
        out_shape=jax.ShapeDtypeStruct((n, d), F32),
        scratch_shapes=[pltpu.VMEM((tm, d), BF16)],
        compiler_params=_params("parallel", "arbitrary"),
        name="ffn_half",
    )(x, g_pre.reshape(1, d), g_post.reshape(1, d), wg, wu, wd)


def _proj_ab_kernel(x_ref, g_ref, w_ref, lng_ref, lnb_ref, zf_ref, zb_ref, h_ref):
    j = pl.program_id(1)

    @pl.when(j == 0)
    def _():
        h_ref[...] = _rms(x_ref[...], g_ref[...]).astype(BF16)

    z = _dot(h_ref[...], w_ref[...])

    @pl.when(j == 0)
    def _():
        zb_ref[...] = (z * np.float32(D_QK ** -0.5 * LOG2E)).astype(BF16)

    @pl.when(jnp.logical_or(j == 1, j == 2))
    def _():
        zf_ref[...] = z
        zb_ref[...] = z.astype(BF16)

    @pl.when(j == 3)
    def _():
        zf_ref[...] = _gelu(z)

    @pl.when(j == 4)
    def _():
        a = _gelu(z)
        c = a - jnp.mean(a, axis=-1, keepdims=True)
        y = c * lax.rsqrt(jnp.mean(c * c, axis=-1, keepdims=True) + EPS)
        zf_ref[...] = y * lng_ref[...] + lnb_ref[...]


def _proj_ab(x, g_pre, w_in, ln_g, ln_b, *, tm=512):
    n, d = x.shape
    w = w_in.shape[1] // 5
    return pl.pallas_call(
        _proj_ab_kernel,
        grid=(n // tm, 5),
        in_specs=[
            pl.BlockSpec((tm, d), lambda i, j: (i, 0)),
            pl.BlockSpec((1, d), lambda i, j: (0, 0)),
            pl.BlockSpec((d, w), lambda i, j: (0, j)),
            pl.BlockSpec((1, w), lambda i, j: (0, 0)),
            pl.BlockSpec((1, w), lambda i, j: (0, 0)),
        ],
        out_specs=[
            pl.BlockSpec((tm, w), lambda i, j: (i, jnp.maximum(j - 1, 0))),
            pl.BlockSpec((tm, w), lambda i, j: (i, jnp.minimum(j, 2))),
        ],
        out_shape=[jax.ShapeDtypeStruct((n, 4 * w), F32), jax.ShapeDtypeStruct((n, 3 * w), BF16)],
        scratch_shapes=[pltpu.VMEM((tm, d), BF16)],
        compiler_params=_params("parallel", "arbitrary"),
        name="proj_ab",
    )(x, g_pre.reshape(1, d), w_in, ln_g.reshape(1, w), ln_b.reshape(1, w))


def _proj_c_kernel(x_ref, g_ref, w_ref, zf_ref, zb_ref, h_ref):
    j = pl.program_id(1)

    @pl.when(j == 0)
    def _():
        h_ref[...] = _rms(x_ref[...], g_ref[...]).astype(BF16)

    z = _dot(h_ref[...], w_ref[...])

    @pl.when(j == 0)
    def _():
        zb_ref[...] = (z * np.float32(D_C ** -0.5)).astype(BF16)

    @pl.when(j > 0)
    def _():
        zf_ref[...] = z
        zb_ref[...] = z.astype(BF16)


def _proj_c(x, g_pre, w_in, *, tm=512):
    n, d = x.shape
    w = w_in.shape[1] // 3
    return pl.pallas_call(
        _proj_c_kernel,
        grid=(n // tm, 3),
        in_specs=[
            pl.BlockSpec((tm, d), lambda i, j: (i, 0)),
            pl.BlockSpec((1, d), lambda i, j: (0, 0)),
            pl.BlockSpec((d, w), lambda i, j: (0, j)),
        ],
        out_specs=[
            pl.BlockSpec((tm, w), lambda i, j: (i, jnp.maximum(j - 1, 0))),
            pl.BlockSpec((tm, w), lambda i, j: (i, j)),
        ],
        out_shape=[jax.ShapeDtypeStruct((n, 2 * w), F32), jax.ShapeDtypeStruct((n, 3 * w), BF16)],
        scratch_shapes=[pltpu.VMEM((tm, d), BF16)],
        compiler_params=_params("parallel", "arbitrary"),
        name="proj_c",
    )(x, g_pre.reshape(1, d), w_in)


def _out_proj_kernel(a_ref, b_ref, wa_ref, wb_ref, x_ref, g_ref, o_ref):
    y = _dot(a_ref[...].astype(BF16), wa_ref[...]) + _dot(b_ref[...].astype(BF16), wb_ref[...])
    o_ref[...] = x_ref[...] + _rms(y, g_ref[...])


def _out_proj(a, b, w_out, x, g_post, *, a_blk=0, b_blk=0, tm=512):
    n, d = x.shape
    kh = w_out.shape[0] // 2
    return pl.pallas_call(
        _out_proj_kernel,
        grid=(n // tm,),
        in_specs=[
            pl.BlockSpec((tm, kh), lambda i: (i, a_blk)),
            pl.BlockSpec((tm, kh), lambda i: (i, b_blk)),
            pl.BlockSpec((kh, d), lambda i: (0, 0)),
            pl.BlockSpec((kh, d), lambda i: (1, 0)),
            pl.BlockSpec((tm, d), lambda i: (i, 0)),
            pl.BlockSpec((1, d), lambda i: (0, 0)),
        ],
        out_specs=pl.BlockSpec((tm, d), lambda i: (i, 0)),
        out_shape=jax.ShapeDtypeStruct((n, d), F32),
        compiler_params=_params("parallel"),
        name="out_proj",
    )(a, b, w_out, w_out, x, g_post.reshape(1, d))


def _bucket_np(dist):
    max_exact = N_BUCKETS // 2
    d = np.maximum(dist, 0)
    ratio = np.log(np.maximum(d, 1).astype(np.float32) / np.float32(max_exact)) / np.float32(math.log(MAX_DISTANCE / max_exact))
    large = np.minimum(max_exact + (ratio * (N_BUCKETS - max_exact)).astype(np.int32), N_BUCKETS - 1)
    return np.where(d < max_exact, d, large).astype(np.int32)


def _bias_expand_kernel(idx_ref, tab_ref, o_ref):
    idx = idx_ref[...]
    val = jnp.zeros(idx.shape, F32)
    for b in range(N_BUCKETS + 1):
        val = jnp.where(idx == b, tab_ref[b:b + 1, :], val)
    o_ref[...] = val


def _bias_expand(idx, tab, *, tr=256):
    gi, r, c = idx.shape
    g = tab.shape[0]
    tr = min(tr, r)
    return pl.pallas_call(
        _bias_expand_kernel,
        grid=(g, r // tr),
        in_specs=[
            pl.BlockSpec((None, tr, c), (lambda a, i: (a, i, 0)) if gi == g else (lambda a, i: (0, i, 0))),
            pl.BlockSpec((None, N_BUCKETS + 1, c), lambda a, i: (a, 0, 0)),
        ],
        out_specs=pl.BlockSpec((None, tr, c), lambda a, i: (a, i, 0)),
        out_shape=jax.ShapeDtypeStruct((g, r, c), F32),
        compiler_params=_params("parallel", "parallel"),
        name="bias_expand",
    )(jnp.asarray(idx), tab)


def _head_table(rel_bias, heads_of_col):
    t = rel_bias.astype(F32)[:, np.asarray(heads_of_col)]
    return jnp.concatenate([t, jnp.full((1, t.shape[1]), NEG_INF, F32)], axis=0)


def _attn_a_prompt_kernel(q_ref, k_ref, v_ref, bias_ref, lam_ref, sg_ref, o_ref, m_ref, l_ref, acc_ref, *, tq, layer):
    qi = pl.program_id(1)
    q = q_ref[...]
    lane = lax.broadcasted_iota(jnp.int32, q.shape, 1)
    zero = jnp.zeros_like(q)
    qs = (jnp.where(lane < D_QK, q, zero), jnp.where(lane >= D_QK, q, zero))
    m_ref[...] = jnp.full(m_ref.shape, NEG_INF, F32)
    l_ref[...] = jnp.zeros(l_ref.shape, F32)
    acc_ref[...] = jnp.zeros(acc_ref.shape, F32)

    def tile(j, bias_tile):
        off = pl.multiple_of(j * tq, tq)
        k = k_ref[pl.ds(off, tq), :]
        v = v_ref[pl.ds(off, tq), :]
        for i in range(2):
            s = _dot_nt(qs[i], k)
            if bias_tile is not None:
                s = s + bias_ref[bias_tile]
            st = [s[:, c * LANE:(c + 1) * LANE] for c in range(tq // LANE)]
            m_prev = m_ref[i]
            m_new = jnp.maximum(m_prev, jnp.max(functools.reduce(jnp.maximum, st), axis=-1, keepdims=True))
            alpha = jnp.exp2(m_prev - m_new)
            ps = [jnp.exp2(t - m_new) for t in st]
            l_ref[i] = alpha * l_ref[i] + functools.reduce(lambda a, b: a + b, ps)
            p = jnp.concatenate([x.astype(BF16) for x in ps], axis=1)
            acc_ref[i] = alpha * acc_ref[i] + _dot(p, v)
            m_ref[i] = m_new

    def far_body(j, carry):
        tile(j, None)
        return carry

    lax.fori_loop(0, qi - 1, far_body, 0)

    @pl.when(qi >= 1)
    def _():
        tile(qi - 1, 1)

    tile(qi, 0)
    lam = _diff_lambda(lam_ref, layer)
    l0 = jnp.sum(l_ref[0], axis=-1, keepdims=True)
    l1 = jnp.sum(l_ref[1], axis=-1, keepdims=True)
    o = acc_ref[0] / l0 - lam * (acc_ref[1] / l1)
    o_ref[...] = (_rms(o, sg_ref[...]) * np.float32(1.0 - _lambda_init(layer))).astype(o_ref.dtype)


def _attn_a_prompt(zb, s_len, bias_tiles, lam_p, subln_g, layer, *, tq=512):
    w = zb.shape[1] // 3
    hw = w // H_A
    return pl.pallas_call(
        functools.partial(_attn_a_prompt_kernel, tq=tq, layer=layer),
        grid=(H_A, s_len // tq),
        in_specs=[
            pl.BlockSpec((tq, hw), lambda h, i: (i, h)),
            pl.BlockSpec((s_len, hw), lambda h, i: (0, H_A + h)),
            pl.BlockSpec((s_len, hw), lambda h, i: (0, 2 * H_A + h)),
            pl.BlockSpec((None, 2, tq, tq), lambda h, i: (h, 0, 0, 0)),
            pl.BlockSpec((4, D_QK), lambda h, i: (0, 0)),
            pl.BlockSpec((1, hw), lambda h, i: (0, 0)),
        ],
        out_specs=pl.BlockSpec((tq, hw), lambda h, i: (i, h)),
        out_shape=jax.ShapeDtypeStruct((s_len, w), BF16),
        scratch_shapes=[pltpu.VMEM((2, tq, LANE), F32), pltpu.VMEM((2, tq, LANE), F32), pltpu.VMEM((2, tq, hw), F32)],
        compiler_params=_params("parallel", "arbitrary"),
        name="attn_a_prompt",
    )(zb, zb, zb, bias_tiles, lam_p, subln_g.reshape(1, hw))


def _bias_tiles_a_prompt(rel_bias, tq):
    assert _bucket_np(np.array([tq + 1]))[0] == N_BUCKETS - 1
    i = np.arange(tq)[:, None]
    j = np.arange(tq)[None, :]
    idx = []
    for delta in (0, tq):
        dist = delta + i - j
        idx.append(np.where(dist >= 0, _bucket_np(dist), MASKED))
    idx = np.stack(idx).reshape(1, 2 * tq, tq).astype(np.int32)
    shifted = (rel_bias - rel_bias[N_BUCKETS - 1:]) * np.float32(LOG2E)
    tab = jnp.stack([_head_table(shifted, np.full(tq, h)) for h in range(H_A)])
    return _bias_expand(idx, tab, tr=min(tq, 256)).reshape(H_A, 2, tq, tq)


def _attn_a_sample_kernel(pt_ref, wq_ref, kn_ref, vn_ref, bp_ref, bn_ref, lam_ref, sg_ref, *rest, n_pages, page, layer):
    k_pages = rest[:n_pages]
    v_pages = rest[n_pages:2 * n_pages]
    o_ref = rest[2 * n_pages]
    s_ref, p_ref = rest[2 * n_pages + 1:]
    wq = wq_ref[...]

    def heads_to_lanes(ref):
        return jnp.concatenate([ref[pl.ds(h, page, stride=H_A), :] for h in range(H_A)], axis=1).astype(BF16)

    for p in range(n_pages):
        s_ref[p * page:(p + 1) * page, :] = _dot(heads_to_lanes(k_pages[p]), wq) + bp_ref[p * page:(p + 1) * page, :]
    s_new = _dot(kn_ref[...], wq) + bn_ref[...]
    s_past = s_ref[...]
    m = jnp.maximum(jnp.max(s_past, axis=0, keepdims=True), jnp.max(s_new, axis=0, keepdims=True))
    e_past = jnp.exp2(s_past - m)
    e_new = jnp.exp2(s_new - m)
    den = jnp.sum(e_past, axis=0, keepdims=True) + jnp.sum(e_new, axis=0, keepdims=True)
    col = lax.broadcasted_iota(jnp.int32, den.shape, 1)
    t_new = kn_ref.shape[0]
    lam = _diff_lambda(lam_ref, layer)
    scale = jnp.where((col // t_new) % 2 == 0, 1.0, -lam) / den
    p_ref[...] = (e_past * scale).astype(BF16)
    acc = _dot_tn((e_new * scale).astype(BF16), vn_ref[...])
    for p in range(n_pages):
        acc = acc + _dot_tn(p_ref[p * page:(p + 1) * page, :], heads_to_lanes(v_pages[p]))
    for h in range(H_A):
        r0 = 2 * h * t_new
        o = acc[r0:r0 + t_new, h * LANE:(h + 1) * LANE] + acc[r0 + t_new:r0 + 2 * t_new, h * LANE:(h + 1) * LANE]
        o_ref[:, h * LANE:(h + 1) * LANE] = (_rms(o, sg_ref[...]) * np.float32(1.0 - _lambda_init(layer))).astype(o_ref.dtype)


def _attn_a_sample(zb_s, cache_k, cache_v, page_table, e, bias_past, bias_new, lam_p, subln_g, layer):
    bsz, t, w3 = zb_s.shape
    w = w3 // 3
    n_pages = page_table.shape[1]
    page = cache_k.shape[2]
    assert 2 * H_A * t == LANE
    qt = jnp.swapaxes(zb_s[:, :, :w], 1, 2)
    rows = np.arange(w)[:, None] // D_QK
    cols = np.arange(LANE)[None, :] // t
    wq = jnp.where(jnp.asarray(rows == cols), jnp.tile(qt, (1, 1, 2 * H_A)), jnp.zeros((), BF16))
    kn = zb_s[:, :, w:2 * w]
    vn = zb_s[:, :, 2 * w:]

    cache_k = cache_k.reshape(cache_k.shape[0], cache_k.shape[1], page * H_A, LANE)
    cache_v = cache_v.reshape(cache_v.shape[0], cache_v.shape[1], page * H_A, LANE)

    def page_spec(p):
        return pl.BlockSpec((None, None, page * H_A, LANE), lambda b, pt: (pt[b * n_pages + p], e, 0, 0))

    grid_spec = pltpu.PrefetchScalarGridSpec(
        num_scalar_prefetch=1,
        grid=(bsz,),
        in_specs=[
            pl.BlockSpec((None, w, LANE), lambda b, pt: (b, 0, 0)),
            pl.BlockSpec((None, t, w), lambda b, pt: (b, 0, 0)),
            pl.BlockSpec((None, t, w), lambda b, pt: (b, 0, 0)),
            pl.BlockSpec((n_pages * page, LANE), lambda b, pt: (0, 0)),
            pl.BlockSpec((t, LANE), lambda b, pt: (0, 0)),
            pl.BlockSpec((4, D_QK), lambda b, pt: (0, 0)),
            pl.BlockSpec((1, LANE), lambda b, pt: (0, 0)),
        ] + [page_spec(p) for p in range(n_pages)] + [page_spec(p) for p in range(n_pages)],
        out_specs=pl.BlockSpec((None, t, w), lambda b, pt: (b, 0, 0)),
        scratch_shapes=[pltpu.VMEM((n_pages * page, LANE), F32), pltpu.VMEM((n_pages * page, LANE), BF16)],
    )
    return pl.pallas_call(
        functools.partial(_attn_a_sample_kernel, n_pages=n_pages, page=page, layer=layer),
        grid_spec=grid_spec,
        out_shape=jax.ShapeDtypeStruct((bsz, t, w), BF16),
        compiler_params=_params("arbitrary"),
        name="attn_a_sample",
    )(page_table.reshape(-1), wq, kn, vn, bias_past, bias_new, lam_p, subln_g.reshape(1, LANE),
      *([cache_k] * n_pages), *([cache_v] * n_pages))


def _bias_a_sample(rel_bias, past, t):
    kpos = np.arange(past + t)[:, None]
    col = np.arange(LANE)[None, :]
    dist = past + (col % t) - kpos
    idx = np.where(dist >= 0, _bucket_np(dist), MASKED).astype(np.int32)
    pad = (-idx.shape[0]) % 8
    idx = np.pad(idx, ((0, pad), (0, 0)), constant_values=MASKED)[None]
    tab = _head_table(rel_bias * np.float32(LOG2E), np.arange(LANE) // (2 * t))[None]
    out = _bias_expand(idx, tab, tr=idx.shape[1])[0]
    return out[:past], out[past:past + t]


def _gate_prompt_kernel(u_ref, g_ref, w_ref, b_ref, o_ref, *, chunks):
    row = lax.broadcasted_iota(jnp.int32, (CHUNK_B, CHUNK_B), 0)
    col = lax.broadcasted_iota(jnp.int32, (CHUNK_B, CHUNK_B), 1)
    for gi in range(G_B):
        wg = jnp.where(row >= col, w_ref[gi], 0.0).astype(BF16)
        bg = b_ref[:, gi:gi + 1]
        for c in range(chunks):
            rs = slice(c * CHUNK_B, (c + 1) * CHUNK_B)
            cs = slice(gi * C_B, (gi + 1) * C_B)
            mixed = _dot(wg, g_ref[rs, cs].astype(BF16)) + bg
            o_ref[rs, cs] = (u_ref[rs, cs] * mixed).astype(o_ref.dtype)


def _gate_prompt(zf, s_len, w_s, b_s, *, chunks=4):
    w = zf.shape[1] // 4
    tm = chunks * CHUNK_B
    return pl.pallas_call(
        functools.partial(_gate_prompt_kernel, chunks=chunks),
        grid=(s_len // tm,),
        in_specs=[
            pl.BlockSpec((tm, w), lambda i: (i, 2)),
            pl.BlockSpec((tm, w), lambda i: (i, 3)),
            pl.BlockSpec((G_B, CHUNK_B, CHUNK_B), lambda i: (0, 0, 0)),
            pl.BlockSpec((CHUNK_B, G_B), lambda i: (0, 0)),
        ],
        out_specs=pl.BlockSpec((tm, w), lambda i: (i, 0)),
        out_shape=jax.ShapeDtypeStruct((s_len, w), BF16),
        compiler_params=_params("parallel"),
        name="gate_prompt",
    )(zf, zf, w_s, b_s.T)


def _gate_sample_kernel(u_ref, g_ref, w_ref, b_ref, o_ref):
    t = u_ref.shape[1]
    g = g_ref[...]
    mixed = jnp.zeros(g.shape, F32) + b_ref[...][None]
    for s in range(t):
        mixed = mixed + w_ref[:, s, :][None] * g[:, s:s + 1, :]
    o_ref[...] = (u_ref[...] * mixed).astype(o_ref.dtype)


def _gate_sample(u, g, w_s, b_s, *, tb=32):
    bsz, t, w = u.shape
    causal = np.tril(np.ones((t, t), bool))
    wt = jnp.where(jnp.asarray(causal)[None], w_s[:, :t, :t], 0.0)
    wt = jnp.repeat(jnp.transpose(wt, (1, 2, 0)), C_B, axis=2)
    bt = jnp.repeat(b_s[:, :t].T, C_B, axis=1)
    return pl.pallas_call(
        _gate_sample_kernel,
        grid=(bsz // tb,),
        in_specs=[
            pl.BlockSpec((tb, t, w), lambda i: (i, 0, 0)),
            pl.BlockSpec((tb, t, w), lambda i: (i, 0, 0)),
            pl.BlockSpec((t, t, w), lambda i: (0, 0, 0)),
            pl.BlockSpec((t, w), lambda i: (0, 0)),
        ],
        out_specs=pl.BlockSpec((tb, t, w), lambda i: (i, 0, 0)),
        out_shape=jax.ShapeDtypeStruct((bsz, t, w), BF16),
        compiler_params=_params("parallel"),
        name="gate_sample",
    )(u, g, wt, bt)


def _attn_c_prompt_kernel(q_ref, kp_ref, kc_ref, vp_ref, vc_ref, bias_ref, o_ref, lse_ref, *, nstep):
    n = pl.program_id(1)
    col = lax.broadcasted_iota(jnp.int32, (1, 2 * nstep), 1)
    no_prev = jnp.where(jnp.logical_and(n == 0, col < nstep), np.float32(NEG_INF), np.float32(0.0))
    for h in range(H_C):
        hs = slice(h * D_C, (h + 1) * D_C)
        k = jnp.concatenate([kp_ref[:, hs], kc_ref[:, hs]], axis=0)
        v = jnp.concatenate([vp_ref[:, hs], vc_ref[:, hs]], axis=0)
        s = _dot_nt(q_ref[:, hs], k) + bias_ref[h] + no_prev
        m = jnp.max(s, axis=-1, keepdims=True)
        e = jnp.exp(s - m)
        den = jnp.sum(e, axis=-1, keepdims=True)
        o_ref[:, hs] = _dot(e.astype(BF16), v) / den
        lse_ref[:, hs] = jnp.broadcast_to(m + jnp.log(den), (nstep, D_C))


def _attn_c_prompt_branch(zb, s_len, dil, nstep, bias):
    n_tok, d3 = zb.shape
    d = d3 // 3
    assert n_tok % dil == 0 and s_len % (dil * nstep) == 0
    z = zb.reshape(n_tok // dil, dil * d3)
    nb = s_len // (dil * nstep)
    prev = lambda part: (lambda r, n: (jnp.maximum(n - 1, 0), 3 * r + part))
    cur = lambda part: (lambda r, n: (n, 3 * r + part))
    out, lse = pl.pallas_call(
        functools.partial(_attn_c_prompt_kernel, nstep=nstep),
        grid=(dil, nb),
        in_specs=[
            pl.BlockSpec((nstep, d), cur(0)),
            pl.BlockSpec((nstep, d), prev(1)),
            pl.BlockSpec((nstep, d), cur(1)),
            pl.BlockSpec((nstep, d), prev(2)),
            pl.BlockSpec((nstep, d), cur(2)),
            pl.BlockSpec((H_C, nstep, 2 * nstep), lambda r, n: (0, 0, 0)),
        ],
        out_specs=[pl.BlockSpec((nstep, d), lambda r, n: (n, r))] * 2,
        out_shape=[jax.ShapeDtypeStruct((s_len // dil, dil * d), F32)] * 2,
        compiler_params=_params("parallel", "arbitrary"),
        name=f"attn_c_prompt_d{dil}",
    )(z, z, z, z, z, bias)
    return out.reshape(s_len, d), lse.reshape(s_len, d)


def _bias_tiles_c_prompt(rel_bias):
    idx = []
    for window, dil in DILATED_BRANCHES:
        nstep = window // dil
        step = nstep + np.arange(nstep)[:, None] - np.arange(2 * nstep)[None, :]
        band = (step >= 0) & (step <= nstep)
        idx.append(np.where(band, _bucket_np(np.clip(step, 0, nstep) * dil), MASKED))
    nstep = idx[0].shape[0]
    assert all(i.shape == (nstep, 2 * nstep) for i in idx)
    nbr = len(idx)
    idx = np.stack(idx).reshape(1, nbr * nstep, 2 * nstep).astype(np.int32)
    tab = jnp.stack([_head_table(rel_bias, np.full(2 * nstep, h)) for h in range(H_C)])
    out = _bias_expand(idx, tab, tr=nstep)
    return jnp.swapaxes(out.reshape(H_C, nbr, nstep, 2 * nstep), 0, 1)


def _combine_kernel(o0, o1, o2, l0, l1, l2, out_ref):
    a0, a1, a2 = l0[...], l1[...], l2[...]
    m = jnp.maximum(jnp.maximum(a0, a1), a2)
    e0, e1, e2 = jnp.exp(a0 - m), jnp.exp(a1 - m), jnp.exp(a2 - m)
    den = e0 + e1 + e2
    out_ref[...] = ((e0 / den) * o0[...] + (e1 / den) * o1[...] + (e2 / den) * o2[...]).astype(out_ref.dtype)


def _combine_branches(outs, lses, *, tm=512):
    s_len, d = outs[0].shape
    spec = pl.BlockSpec((tm, d), lambda i: (i, 0))
    return pl.pallas_call(
        _combine_kernel,
        grid=(s_len // tm,),
        in_specs=[spec] * 6,
        out_specs=spec,
        out_shape=jax.ShapeDtypeStruct((s_len, d), BF16),
        compiler_params=_params("parallel"),
        name="combine_c",
    )(*outs, *lses)


def _attn_c_sample_kernel(wq_ref, kn_ref, vn_ref, bp_ref, bn_ref, kc_ref, vc_ref, o_ref, s_ref, p_ref, pn_ref, acc_ref, *, nck, ck):
    c = pl.program_id(1)
    t_new = kn_ref.shape[0]

    def heads_to_lanes(ref):
        return jnp.concatenate([ref[pl.ds(h, ck, stride=H_C), :] for h in range(H_C)], axis=1).astype(BF16)

    @pl.when(c < nck)
    def _():
        off = pl.multiple_of(c * ck, ck)
        s_ref[pl.ds(off, ck), :] = _dot(heads_to_lanes(kc_ref), wq_ref[...])

    @pl.when(c == nck - 1)
    def _():
        s_past = s_ref[...]
        s_new = _dot(kn_ref[...], wq_ref[...])
        es, lses = [], []
        for b in range(bp_ref.shape[0]):
            sp = s_past + bp_ref[b]
            sn = s_new + bn_ref[b]
            m = jnp.maximum(jnp.max(sp, axis=0, keepdims=True), jnp.max(sn, axis=0, keepdims=True))
            ep = jnp.exp(sp - m)
            en = jnp.exp(sn - m)
            den = jnp.sum(ep, axis=0, keepdims=True) + jnp.sum(en, axis=0, keepdims=True)
            es.append((ep, en, den))
            lses.append(m + jnp.log(den))
        mm = functools.reduce(jnp.maximum, lses)
        ws = [jnp.exp(l - mm) for l in lses]
        wsum = functools.reduce(lambda a, b: a + b, ws)
        pp = jnp.zeros(s_past.shape, F32)
        pn = jnp.zeros(s_new.shape, F32)
        for (ep, en, den), w in zip(es, ws):
            coef = (w / wsum) / den
            pp = pp + coef * ep
            pn = pn + coef * en
        p_ref[...] = pp.astype(BF16)
        pn_ref[...] = pn.astype(BF16)

    @pl.when(c == nck)
    def _():
        acc_ref[...] = _dot_tn(pn_ref[...], vn_ref[...])

    @pl.when(c >= nck)
    def _():
        off = pl.multiple_of((c - nck) * ck, ck)
        acc_ref[...] += _dot_tn(p_ref[pl.ds(off, ck), :], heads_to_lanes(vc_ref))

    @pl.when(c == 2 * nck - 1)
    def _():
        for h in range(H_C):
            o_ref[:, h * D_C:(h + 1) * D_C] = acc_ref[h * t_new:(h + 1) * t_new, h * D_C:(h + 1) * D_C].astype(o_ref.dtype)


def _attn_c_sample(zb_s, cache_k, cache_v, o_idx, bias_past, bias_new, *, ck=512):
    bsz, t, d3 = zb_s.shape
    d = d3 // 3
    wlen = cache_k.shape[2]
    nck = wlen // ck
    nbr = bias_past.shape[0]
    assert H_C * t == LANE and wlen % ck == 0
    qt = jnp.swapaxes(zb_s[:, :, :d], 1, 2)
    rows = np.arange(d)[:, None] // D_C
    cols = np.arange(LANE)[None, :] // t
    wq = jnp.where(jnp.asarray(rows == cols), jnp.tile(qt, (1, 1, H_C)), jnp.zeros((), BF16))
    kn = zb_s[:, :, d:2 * d]
    vn = zb_s[:, :, 2 * d:]
    cache_k = cache_k.reshape(cache_k.shape[0], bsz, wlen * H_C, D_C)
    cache_v = cache_v.reshape(cache_v.shape[0], bsz, wlen * H_C, D_C)
    return pl.pallas_call(
        functools.partial(_attn_c_sample_kernel, nck=nck, ck=ck),
        grid=(bsz, 2 * nck),
        in_specs=[
            pl.BlockSpec((None, d, LANE), lambda b, c: (b, 0, 0)),
            pl.BlockSpec((None, t, d), lambda b, c: (b, 0, 0)),
            pl.BlockSpec((None, t, d), lambda b, c: (b, 0, 0)),
            pl.BlockSpec((nbr, wlen, LANE), lambda b, c: (0, 0, 0)),
            pl.BlockSpec((nbr, t, LANE), lambda b, c: (0, 0, 0)),
            pl.BlockSpec((None, None, ck * H_C, D_C), lambda b, c: (o_idx, b, jnp.minimum(c, nck - 1), 0)),
            pl.BlockSpec((None, None, ck * H_C, D_C), lambda b, c: (o_idx, b, jnp.maximum(c - nck, 0), 0)),
        ],
        out_specs=pl.BlockSpec((None, t, d), lambda b, c: (b, 0, 0)),
        out_shape=jax.ShapeDtypeStruct((bsz, t, d), BF16),
        scratch_shapes=[pltpu.VMEM((wlen, LANE), F32), pltpu.VMEM((wlen, LANE), BF16),
                        pltpu.VMEM((t, LANE), BF16), pltpu.VMEM((LANE, d), F32)],
        compiler_params=_params("parallel", "arbitrary"),
        name="attn_c_sample",
    )(wq, kn, vn, bias_past, bias_new, cache_k, cache_v)


def _bias_c_sample(rel_bias, wlen, t):
    krow = np.arange(wlen + t)[:, None]
    col = np.arange(LANE)[None, :]
    dist = wlen + (col % t) - krow
    idx = []
    for window, dil in DILATED_BRANCHES:
        ok = (dist >= 0) & (dist % dil == 0) & (dist <= window)
        idx.append(np.where(ok, _bucket_np(dist), MASKED))
    idx = np.stack(idx).astype(np.int32)
    pad = (-idx.shape[1]) % 8
    idx = np.pad(idx, ((0, 0), (0, pad), (0, 0)), constant_values=MASKED)
    tab = jnp.broadcast_to(_head_table(rel_bias, np.arange(LANE) // t)[None], (idx.shape[0], N_BUCKETS + 1, LANE))
    out = _bias_expand(idx, tab, tr=idx.shape[1])
    return out[:, :wlen], out[:, wlen:wlen + t]


def kernel(x_prompt, x_sample, cache_k_a, cache_v_a, page_table, cache_k_c, cache_v_c, rel_bias, norm_gains, w_ffn_gate, w_ffn_up, w_ffn_down, w_in_ab, w_out_ab, lambda_qk, subln_gain, ln_v_gain, ln_v_bias, w_spatial, b_spatial, w_in_c, w_out_c):
    bp, s_len, d = x_prompt.shape
    bs, t_new, _ = x_sample.shape
    assert bp == 1
    n_s = bs * t_new
    depth = norm_gains.shape[0]
    x = jnp.concatenate([x_prompt.reshape(s_len, d), x_sample.reshape(n_s, d)], axis=0)

    tq_a = 512
    past_a = page_table.shape[1] * cache_k_a.shape[2]
    bias_a_prompt = _bias_tiles_a_prompt(rel_bias, tq_a)
    bias_a_past, bias_a_new = _bias_a_sample(rel_bias, past_a, t_new)
    bias_c_prompt = _bias_tiles_c_prompt(rel_bias)
    bias_c_past, bias_c_new = _bias_c_sample(rel_bias, cache_k_c.shape[2], t_new)

    ka_p, va_p, ka_s, va_s, vb_p, vb_s = [], [], [], [], [], []
    kc_p, vc_p, kc_s, vc_s = [], [], [], []
    for li in range(depth):
        ng = norm_gains[li]

        def half_ffn(x, j):
            return _ffn_half(x, ng[2 * j, 0], ng[2 * j, 1], w_ffn_gate[li, j].astype(BF16),
                             w_ffn_up[li, j].astype(BF16), w_ffn_down[li, j].astype(BF16))

        x = half_ffn(x, 0)
        if li % 2 == 0:
            e = li // 2
            w = H_A * 2 * D_QK
            zf, zb = _proj_ab(x, ng[1, 0], w_in_ab[e].astype(BF16), ln_v_gain[e], ln_v_bias[e])
            ka_p.append(zf[:s_len, :w].reshape(1, s_len, H_A, 2 * D_QK))
            va_p.append(zf[:s_len, w:2 * w].reshape(1, s_len, H_A, 2 * D_QK))
            ka_s.append(zf[s_len:, :w].reshape(bs, t_new, H_A, 2 * D_QK))
            va_s.append(zf[s_len:, w:2 * w].reshape(bs, t_new, H_A, 2 * D_QK))
            last = s_len - ((s_len - 1) // CHUNK_B) * CHUNK_B
            vb_p.append(zf[s_len - last:s_len, 3 * w:].reshape(1, last, w))
            vb_s.append(zf[s_len:, 3 * w:].reshape(bs, t_new, w))

            o_p = _attn_a_prompt(zb, s_len, bias_a_prompt, lambda_qk[e], subln_gain[e], li, tq=tq_a)
            o_s = _attn_a_sample(zb[s_len:].reshape(bs, t_new, 3 * w), cache_k_a, cache_v_a, page_table, e,
                                 bias_a_past, bias_a_new, lambda_qk[e], subln_gain[e], li)
            g_p = _gate_prompt(zf, s_len, w_spatial[e], b_spatial[e])
            g_s = _gate_sample(zf[s_len:, 2 * w:3 * w].reshape(bs, t_new, w), zf[s_len:, 3 * w:].reshape(bs, t_new, w),
                               w_spatial[e], b_spatial[e])
            a = jnp.concatenate([o_p, o_s.reshape(n_s, w)], axis=0)
            b = jnp.concatenate([g_p, g_s.reshape(n_s, w)], axis=0)
            x = _out_proj(a, b, w_out_ab[e].astype(BF16), x, ng[1, 1])
        else:
            o = li // 2
            zf, zb = _proj_c(x, ng[1, 0], w_in_c[o].astype(BF16))
            keep = min(max(wd for wd, _ in DILATED_BRANCHES), s_len)
            kc_p.append(zf[s_len - keep:s_len, :d].reshape(1, keep, H_C, D_C))
            vc_p.append(zf[s_len - keep:s_len, d:].reshape(1, keep, H_C, D_C))
            kc_s.append(zf[s_len:, :d].reshape(bs, t_new, H_C, D_C))
            vc_s.append(zf[s_len:, d:].reshape(bs, t_new, H_C, D_C))
            outs, lses = [], []
            for bi, (window, dil) in enumerate(DILATED_BRANCHES):
                ob, lb = _attn_c_prompt_branch(zb, s_len, dil, window // dil, bias_c_prompt[bi])
                outs.append(ob)
                lses.append(lb)
            o_p = _combine_branches(outs, lses)
            o_s = _attn_c_sample(zb[s_len:].reshape(bs, t_new, 3 * d), cache_k_c, cache_v_c, o, bias_c_past, bias_c_new)
            ab = jnp.concatenate([o_p, o_s.reshape(n_s, d)], axis=0)
            x = _out_proj(ab, ab, w_out_c[o].astype(BF16), x, ng[1, 1], a_blk=0, b_blk=1)
        x = half_ffn(x, 1)

    y_prompt = x[:s_len].reshape(1, s_len, d)
    y_sample = x[s_len:].reshape(bs, t_new, d)
    return (y_prompt, y_sample,
            jnp.stack(ka_p, axis=1), jnp.stack(va_p, axis=1), jnp.stack(ka_s, axis=1), jnp.stack(va_s, axis=1),
            jnp.stack(vb_p, axis=0), jnp.stack(vb_s, axis=0),
            jnp.stack(kc_p, axis=0), jnp.stack(vc_p, axis=0), jnp.stack(kc_s, axis=0), jnp.stack(vc_s, axis=0))
```

```python
import functools
import math

import numpy as np
import jax
import jax.numpy as jnp
from jax import lax
from jax.experimental import pallas as pl
from jax.experimental.pallas import tpu as pltpu

F32 = jnp.float32
BF16 = jnp.bfloat16

EPS = 1e-6
NEG_INF = -1e30
LOG2E = math.log2(math.e)
LANE = 128
VMEM_LIMIT = 56 * 1024 * 1024

H_A = 8
D_QK = 64
G_B = 8
C_B = 128
CHUNK_B = 128
H_C = 16
D_C = 128
DILATED_BRANCHES = ((128, 1), (512, 4), (2048, 16))
N_BUCKETS = 32
MAX_DISTANCE = 128
MASKED = N_BUCKETS


def _params(*sem):
    return pltpu.CompilerParams(dimension_semantics=sem, vmem_limit_bytes=VMEM_LIMIT)


def _rms(x, g):
    return x * lax.rsqrt(jnp.mean(x * x, axis=-1, keepdims=True) + EPS) * g


def _gelu(x):
    return 0.5 * x * (1.0 + lax.erf(x * np.float32(math.sqrt(0.5))))


def _dot(a, b):
    return jnp.dot(a, b, preferred_element_type=F32)


def _dot_nt(a, b):
    return lax.dot_general(a, b, (((1,), (1,)), ((), ())), preferred_element_type=F32)


def _dot_tn(a, b):
    return lax.dot_general(a, b, (((0,), (0,)), ((), ())), preferred_element_type=F32)


def _lambda_init(layer):
    return 0.8 - 0.6 * math.exp(-0.3 * layer)


def _diff_lambda(lam_ref, layer):
    lp = lam_ref[...]
    a = jnp.sum(lp[0:1] * lp[1:2], axis=-1, keepdims=True)
    b = jnp.sum(lp[2:3] * lp[3:4], axis=-1, keepdims=True)
    return jnp.exp(a) - jnp.exp(b) + np.float32(_lambda_init(layer))


def _ffn_kernel(x_ref, gpre_ref, gpost_ref, wg_ref, wu_ref, wd_ref, o_ref, h_ref):
    f = pl.program_id(1)

    @pl.when(f == 0)
    def _():
        h_ref[...] = _rms(x_ref[...], gpre_ref[...]).astype(BF16)
        o_ref[...] = jnp.zeros_like(o_ref)

    h = h_ref[...]
    a = _dot(h, wg_ref[...])
    b = _dot(h, wu_ref[...])
    s = (a * jax.nn.sigmoid(a)) * b
    o_ref[...] += _dot(s.astype(BF16), wd_ref[...])

    @pl.when(f == pl.num_programs(1) - 1)
    def _():
        o_ref[...] = x_ref[...] + 0.5 * _rms(o_ref[...], gpost_ref[...])


def _ffn_half(x, g_pre, g_post, wg, wu, wd, *, tm=512, tf=512):
    n, d = x.shape
    f = wg.shape[1]
    return pl.pallas_call(
        _ffn_kernel,
        grid=(n // tm, f // tf),
        in_specs=[
            pl.BlockSpec((tm, d), lambda i, j: (i, 0)),
            pl.BlockSpec((1, d), lambda i, j: (0, 0)),
            pl.BlockSpec((1, d), lambda i, j: (0, 0)),
            pl.BlockSpec((d, tf), lambda i, j: (0, j)),
            pl.BlockSpec((d, tf), lambda i, j: (0, j)),
            pl.BlockSpec((tf, d), lambda i, j: (j, 0)),
        ],
        out_specs=pl.BlockSpec((tm, d), lambda i, j: (i, 0)),
        out_shape=jax.ShapeDtypeStruct((n, d), F32),
        scratch_shapes=[pltpu.VMEM((tm, d), BF16)],
        compiler_params=_params("parallel", "arbitrary"),
        name="ffn_half",
    )(x, g_pre.reshape(1, d), g_post.reshape(1, d), wg, wu, wd)


def _proj_ab_kernel(x_ref, g_ref, w_ref, lng_ref, lnb_ref, k_ref, v_ref, u_ref, gv_ref, zb_ref, h_ref):
    j = pl.program_id(1)

    @pl.when(j == 0)
    def _():
        h_ref[...] = _rms(x_ref[...], g_ref[...]).astype(BF16)

    z = _dot(h_ref[...], w_ref[...])

    @pl.when(j == 0)
    def _():
        zb_ref[...] = (z * np.float32(D_QK ** -0.5 * LOG2E)).astype(BF16)

    @pl.when(j == 1)
    def _():
        k_ref[...] = z
        zb_ref[...] = z.astype(BF16)

    @pl.when(j == 2)
    def _():
        v_ref[...] = z
        zb_ref[...] = z.astype(BF16)

    @pl.when(j == 3)
    def _():
        u_ref[...] = _gelu(z)

    @pl.when(j == 4)
    def _():
        a = _gelu(z)
        c = a - jnp.mean(a, axis=-1, keepdims=True)
        y = c * lax.rsqrt(jnp.mean(c * c, axis=-1, keepdims=True) + EPS)
        gv_ref[...] = y * lng_ref[...] + lnb_ref[...]


def _proj_ab(x, g_pre, w_in, ln_g, ln_b, *, tm=512):
    n, d = x.shape
    w = w_in.shape[1] // 5
    row = pl.BlockSpec((tm, w), lambda i, j: (i, 0))
    return pl.pallas_call(
        _proj_ab_kernel,
        grid=(n // tm, 5),
        in_specs=[
            pl.BlockSpec((tm, d), lambda i, j: (i, 0)),
            pl.BlockSpec((1, d), lambda i, j: (0, 0)),
            pl.BlockSpec((d, w), lambda i, j: (0, j)),
            pl.BlockSpec((1, w), lambda i, j: (0, 0)),
            pl.BlockSpec((1, w), lambda i, j: (0, 0)),
        ],
        out_specs=[row, row, row, row, pl.BlockSpec((tm, w), lambda i, j: (i, jnp.minimum(j, 2)))],
        out_shape=[jax.ShapeDtypeStruct((n, w), F32)] * 4 + [jax.ShapeDtypeStruct((n, 3 * w), BF16)],
        scratch_shapes=[pltpu.VMEM((tm, d), BF16)],
        compiler_params=_params("parallel", "arbitrary"),
        name="proj_ab",
    )(x, g_pre.reshape(1, d), w_in, ln_g.reshape(1, w), ln_b.reshape(1, w))


def _proj_c_kernel(x_ref, g_ref, w_ref, zf_ref, h_ref):
    j = pl.program_id(1)

    @pl.when(j == 0)
    def _():
        h_ref[...] = _rms(x_ref[...], g_ref[...]).astype(BF16)

    z = _dot(h_ref[...], w_ref[...])

    @pl.when(j == 0)
    def _():
        zf_ref[...] = z * np.float32(D_C ** -0.5)

    @pl.when(j > 0)
    def _():
        zf_ref[...] = z


def _proj_c(x, g_pre, w_in, *, tm=512):
    n, d = x.shape
    w = w_in.shape[1] // 3
    return pl.pallas_call(
        _proj_c_kernel,
        grid=(n // tm, 3),
        in_specs=[
            pl.BlockSpec((tm, d), lambda i, j: (i, 0)),
            pl.BlockSpec((1, d), lambda i, j: (0, 0)),
            pl.BlockSpec((d, w), lambda i, j: (0, j)),
        ],
        out_specs=pl.BlockSpec((tm, w), lambda i, j: (i, j)),
        out_shape=jax.ShapeDtypeStruct((n, 3 * w), F32),
        scratch_shapes=[pltpu.VMEM((tm, d), BF16)],
        compiler_params=_params("parallel", "arbitrary"),
        name="proj_c",
    )(x, g_pre.reshape(1, d), w_in)


def _out_proj_kernel(a_ref, b_ref, wa_ref, wb_ref, x_ref, g_ref, o_ref):
    y = _dot(a_ref[...].astype(BF16), wa_ref[...]) + _dot(b_ref[...].astype(BF16), wb_ref[...])
    o_ref[...] = x_ref[...] + _rms(y, g_ref[...])


def _out_proj(a, b, w_out, x, g_post, *, a_blk=0, b_blk=0, tm=512):
    n, d = x.shape
    kh = w_out.shape[0] // 2
    return pl.pallas_call(
        _out_proj_kernel,
        grid=(n // tm,),
        in_specs=[
            pl.BlockSpec((tm, kh), lambda i: (i, a_blk)),
            pl.BlockSpec((tm, kh), lambda i: (i, b_blk)),
            pl.BlockSpec((kh, d), lambda i: (0, 0)),
            pl.BlockSpec((kh, d), lambda i: (1, 0)),
            pl.BlockSpec((tm, d), lambda i: (i, 0)),
            pl.BlockSpec((1, d), lambda i: (0, 0)),
        ],
        out_specs=pl.BlockSpec((tm, d), lambda i: (i, 0)),
        out_shape=jax.ShapeDtypeStruct((n, d), F32),
        compiler_params=_params("parallel"),
        name="out_proj",
    )(a, b, w_out, w_out, x, g_post.reshape(1, d))


def _bucket_np(dist):
    max_exact = N_BUCKETS // 2
    d = np.maximum(dist, 0)
    ratio = np.log(np.maximum(d, 1).astype(np.float32) / np.float32(max_exact)) / np.float32(math.log(MAX_DISTANCE / max_exact))
    large = np.minimum(max_exact + (ratio * (N_BUCKETS - max_exact)).astype(np.int32), N_BUCKETS - 1)
    return np.where(d < max_exact, d, large).astype(np.int32)


def _bias_expand_kernel(idx_ref, tab_ref, o_ref):
    idx = idx_ref[...]
    val = jnp.zeros(idx.shape, F32)
    for b in range(N_BUCKETS + 1):
        val = jnp.where(idx == b, tab_ref[b:b + 1, :], val)
    o_ref[...] = val


def _bias_expand(idx, tab, *, tr=256):
    gi, r, c = idx.shape
    g = tab.shape[0]
    tr = min(tr, r)
    return pl.pallas_call(
        _bias_expand_kernel,
        grid=(g, r // tr),
        in_specs=[
            pl.BlockSpec((None, tr, c), (lambda a, i: (a, i, 0)) if gi == g else (lambda a, i: (0, i, 0))),
            pl.BlockSpec((None, N_BUCKETS + 1, c), lambda a, i: (a, 0, 0)),
        ],
        out_specs=pl.BlockSpec((None, tr, c), lambda a, i: (a, i, 0)),
        out_shape=jax.ShapeDtypeStruct((g, r, c), F32),
        compiler_params=_params("parallel", "parallel"),
        name="bias_expand",
    )(jnp.asarray(idx), tab)


def _head_table(rel_bias, heads_of_col):
    t = rel_bias.astype(F32)[:, np.asarray(heads_of_col)]
    return jnp.concatenate([t, jnp.full((1, t.shape[1]), NEG_INF, F32)], axis=0)


def _attn_a_prompt_kernel(q_ref, k_ref, v_ref, bias_ref, lam_ref, sg_ref, o_ref, m_ref, l_ref, acc_ref, *, tq, layer):
    qi = pl.program_id(1)
    q = q_ref[...]
    lane = lax.broadcasted_iota(jnp.int32, q.shape, 1)
    zero = jnp.zeros_like(q)
    qs = (jnp.where(lane < D_QK, q, zero), jnp.where(lane >= D_QK, q, zero))
    m_ref[...] = jnp.full(m_ref.shape, NEG_INF, F32)
    l_ref[...] = jnp.zeros(l_ref.shape, F32)
    acc_ref[...] = jnp.zeros(acc_ref.shape, F32)

    def tile(j, bias_tile):
        off = pl.multiple_of(j * tq, tq)
        k = k_ref[pl.ds(off, tq), :]
        v = v_ref[pl.ds(off, tq), :]
        for i in range(2):
            s = _dot_nt(qs[i], k)
            if bias_tile is not None:
                s = s + bias_ref[bias_tile]
            st = [s[:, c * LANE:(c + 1) * LANE] for c in range(tq // LANE)]
            m_prev = m_ref[i]
            m_new = jnp.maximum(m_prev, jnp.max(functools.reduce(jnp.maximum, st), axis=-1, keepdims=True))
            alpha = jnp.exp2(m_prev - m_new)
            ps = [jnp.exp2(t - m_new) for t in st]
            l_ref[i] = alpha * l_ref[i] + functools.reduce(lambda a, b: a + b, ps)
            p = jnp.concatenate([x.astype(BF16) for x in ps], axis=1)
            acc_ref[i] = alpha * acc_ref[i] + _dot(p, v)
            m_ref[i] = m_new

    def far_body(j, carry):
        tile(j, None)
        return carry

    lax.fori_loop(0, qi - 1, far_body, 0)

    @pl.when(qi >= 1)
    def _():
        tile(qi - 1, 1)

    tile(qi, 0)
    lam = _diff_lambda(lam_ref, layer)
    l0 = jnp.sum(l_ref[0], axis=-1, keepdims=True)
    l1 = jnp.sum(l_ref[1], axis=-1, keepdims=True)
    o = acc_ref[0] / l0 - lam * (acc_ref[1] / l1)
    o_ref[...] = (_rms(o, sg_ref[...]) * np.float32(1.0 - _lambda_init(layer))).astype(o_ref.dtype)


def _attn_a_prompt(zb, bias_tiles, lam_p, subln_g, layer, *, tq=512):
    s_len = zb.shape[0]
    w = zb.shape[1] // 3
    hw = w // H_A
    return pl.pallas_call(
        functools.partial(_attn_a_prompt_kernel, tq=tq, layer=layer),
        grid=(H_A, s_len // tq),
        in_specs=[
            pl.BlockSpec((tq, hw), lambda h, i: (i, h)),
            pl.BlockSpec((s_len, hw), lambda h, i: (0, H_A + h)),
            pl.BlockSpec((s_len, hw), lambda h, i: (0, 2 * H_A + h)),
            pl.BlockSpec((None, 2, tq, tq), lambda h, i: (h, 0, 0, 0)),
            pl.BlockSpec((4, D_QK), lambda h, i: (0, 0)),
            pl.BlockSpec((1, hw), lambda h, i: (0, 0)),
        ],
        out_specs=pl.BlockSpec((tq, hw), lambda h, i: (i, h)),
        out_shape=jax.ShapeDtypeStruct((s_len, w), BF16),
        scratch_shapes=[pltpu.VMEM((2, tq, LANE), F32), pltpu.VMEM((2, tq, LANE), F32), pltpu.VMEM((2, tq, hw), F32)],
        compiler_params=_params("parallel", "arbitrary"),
        name="attn_a_prompt",
    )(zb, zb, zb, bias_tiles, lam_p, subln_g.reshape(1, hw))


def _bias_tiles_a_prompt(rel_bias, tq):
    assert _bucket_np(np.array([tq + 1]))[0] == N_BUCKETS - 1
    i = np.arange(tq)[:, None]
    j = np.arange(tq)[None, :]
    idx = []
    for delta in (0, tq):
        dist = delta + i - j
        idx.append(np.where(dist >= 0, _bucket_np(dist), MASKED))
    idx = np.stack(idx).reshape(1, 2 * tq, tq).astype(np.int32)
    shifted = (rel_bias - rel_bias[N_BUCKETS - 1:]) * np.float32(LOG2E)
    tab = jnp.stack([_head_table(shifted, np.full(tq, h)) for h in range(H_A)])
    return _bias_expand(idx, tab, tr=min(tq, 256)).reshape(H_A, 2, tq, tq)


def _attn_a_sample_kernel(pt_ref, wq_ref, kn_ref, vn_ref, bp_ref, bn_ref, lam_ref, sg_ref, *rest, n_pages, page, layer):
    k_pages = rest[:n_pages]
    v_pages = rest[n_pages:2 * n_pages]
    o_ref = rest[2 * n_pages]
    s_ref, p_ref = rest[2 * n_pages + 1:]
    wq = wq_ref[...]

    def heads_to_lanes(ref):
        return jnp.concatenate([ref[pl.ds(h, page, stride=H_A), :] for h in range(H_A)], axis=1).astype(BF16)

    for p in range(n_pages):
        s_ref[p * page:(p + 1) * page, :] = _dot(heads_to_lanes(k_pages[p]), wq) + bp_ref[p * page:(p + 1) * page, :]
    s_new = _dot(kn_ref[...], wq) + bn_ref[...]
    s_past = s_ref[...]
    m = jnp.maximum(jnp.max(s_past, axis=0, keepdims=True), jnp.max(s_new, axis=0, keepdims=True))
    e_past = jnp.exp2(s_past - m)
    e_new = jnp.exp2(s_new - m)
    den = jnp.sum(e_past, axis=0, keepdims=True) + jnp.sum(e_new, axis=0, keepdims=True)
    col = lax.broadcasted_iota(jnp.int32, den.shape, 1)
    t_new = kn_ref.shape[0]
    lam = _diff_lambda(lam_ref, layer)
    scale = jnp.where((col // t_new) % 2 == 0, 1.0, -lam) / den
    p_ref[...] = (e_past * scale).astype(BF16)
    acc = _dot_tn((e_new * scale).astype(BF16), vn_ref[...])
    for p in range(n_pages):
        acc = acc + _dot_tn(p_ref[p * page:(p + 1) * page, :], heads_to_lanes(v_pages[p]))
    for h in range(H_A):
        r0 = 2 * h * t_new
        o = acc[r0:r0 + t_new, h * LANE:(h + 1) * LANE] + acc[r0 + t_new:r0 + 2 * t_new, h * LANE:(h + 1) * LANE]
        o_ref[:, h * LANE:(h + 1) * LANE] = (_rms(o, sg_ref[...]) * np.float32(1.0 - _lambda_init(layer))).astype(o_ref.dtype)


def _attn_a_sample(zb_s, cache_k, cache_v, page_table, e, bias_past, bias_new, lam_p, subln_g, layer):
    bsz, t, w3 = zb_s.shape
    w = w3 // 3
    n_pages = page_table.shape[1]
    page = cache_k.shape[2]
    assert 2 * H_A * t == LANE
    qt = jnp.swapaxes(zb_s[:, :, :w], 1, 2)
    rows = np.arange(w)[:, None] // D_QK
    cols = np.arange(LANE)[None, :] // t
    wq = jnp.where(jnp.asarray(rows == cols), jnp.tile(qt, (1, 1, 2 * H_A)), jnp.zeros((), BF16))
    kn = zb_s[:, :, w:2 * w]
    vn = zb_s[:, :, 2 * w:]

    cache_k = cache_k.reshape(cache_k.shape[0], cache_k.shape[1], page * H_A, LANE)
    cache_v = cache_v.reshape(cache_v.shape[0], cache_v.shape[1], page * H_A, LANE)

    def page_spec(p):
        return pl.BlockSpec((None, None, page * H_A, LANE), lambda b, pt: (pt[b * n_pages + p], e, 0, 0))

    grid_spec = pltpu.PrefetchScalarGridSpec(
        num_scalar_prefetch=1,
        grid=(bsz,),
        in_specs=[
            pl.BlockSpec((None, w, LANE), lambda b, pt: (b, 0, 0)),
            pl.BlockSpec((None, t, w), lambda b, pt: (b, 0, 0)),
            pl.BlockSpec((None, t, w), lambda b, pt: (b, 0, 0)),
            pl.BlockSpec((n_pages * page, LANE), lambda b, pt: (0, 0)),
            pl.BlockSpec((t, LANE), lambda b, pt: (0, 0)),
            pl.BlockSpec((4, D_QK), lambda b, pt: (0, 0)),
            pl.BlockSpec((1, LANE), lambda b, pt: (0, 0)),
        ] + [page_spec(p) for p in range(n_pages)] + [page_spec(p) for p in range(n_pages)],
        out_specs=pl.BlockSpec((None, t, w), lambda b, pt: (b, 0, 0)),
        scratch_shapes=[pltpu.VMEM((n_pages * page, LANE), F32), pltpu.VMEM((n_pages * page, LANE), BF16)],
    )
    return pl.pallas_call(
        functools.partial(_attn_a_sample_kernel, n_pages=n_pages, page=page, layer=layer),
        grid_spec=grid_spec,
        out_shape=jax.ShapeDtypeStruct((bsz, t, w), BF16),
        compiler_params=_params("arbitrary"),
        name="attn_a_sample",
    )(page_table.reshape(-1), wq, kn, vn, bias_past, bias_new, lam_p, subln_g.reshape(1, LANE),
      *([cache_k] * n_pages), *([cache_v] * n_pages))


def _bias_a_sample(rel_bias, past, t):
    kpos = np.arange(past + t)[:, None]
    col = np.arange(LANE)[None, :]
    dist = past + (col % t) - kpos
    idx = np.where(dist >= 0, _bucket_np(dist), MASKED).astype(np.int32)
    pad = (-idx.shape[0]) % 8
    idx = np.pad(idx, ((0, pad), (0, 0)), constant_values=MASKED)[None]
    tab = _head_table(rel_bias * np.float32(LOG2E), np.arange(LANE) // (2 * t))[None]
    out = _bias_expand(idx, tab, tr=idx.shape[1])[0]
    return out[:past], out[past:past + t]


def _gate_prompt_kernel(u_ref, g_ref, w_ref, b_ref, o_ref, *, chunks):
    row = lax.broadcasted_iota(jnp.int32, (CHUNK_B, CHUNK_B), 0)
    col = lax.broadcasted_iota(jnp.int32, (CHUNK_B, CHUNK_B), 1)
    for gi in range(G_B):
        wg = jnp.where(row >= col, w_ref[gi], 0.0).astype(BF16)
        bg = b_ref[:, gi:gi + 1]
        for c in range(chunks):
            rs = slice(c * CHUNK_B, (c + 1) * CHUNK_B)
            cs = slice(gi * C_B, (gi + 1) * C_B)
            mixed = _dot(wg, g_ref[rs, cs].astype(BF16)) + bg
            o_ref[rs, cs] = (u_ref[rs, cs] * mixed).astype(o_ref.dtype)


def _gate_prompt(u, g, w_s, b_s, *, chunks=4):
    s_len, w = u.shape
    tm = chunks * CHUNK_B
    return pl.pallas_call(
        functools.partial(_gate_prompt_kernel, chunks=chunks),
        grid=(s_len // tm,),
        in_specs=[
            pl.BlockSpec((tm, w), lambda i: (i, 0)),
            pl.BlockSpec((tm, w), lambda i: (i, 0)),
            pl.BlockSpec((G_B, CHUNK_B, CHUNK_B), lambda i: (0, 0, 0)),
            pl.BlockSpec((CHUNK_B, G_B), lambda i: (0, 0)),
        ],
        out_specs=pl.BlockSpec((tm, w), lambda i: (i, 0)),
        out_shape=jax.ShapeDtypeStruct((s_len, w), BF16),
        compiler_params=_params("parallel"),
        name="gate_prompt",
    )(u, g, w_s, b_s.T)


def _gate_sample_kernel(u_ref, g_ref, w_ref, b_ref, o_ref):
    t = u_ref.shape[1]
    g = g_ref[...]
    mixed = jnp.zeros(g.shape, F32) + b_ref[...][None]
    for s in range(t):
        mixed = mixed + w_ref[:, s, :][None] * g[:, s:s + 1, :]
    o_ref[...] = (u_ref[...] * mixed).astype(o_ref.dtype)


def _gate_sample(u, g, w_s, b_s, *, tb=32):
    bsz, t, w = u.shape
    causal = np.tril(np.ones((t, t), bool))
    wt = jnp.where(jnp.asarray(causal)[None], w_s[:, :t, :t], 0.0)
    wt = jnp.repeat(jnp.transpose(wt, (1, 2, 0)), C_B, axis=2)
    bt = jnp.repeat(b_s[:, :t].T, C_B, axis=1)
    return pl.pallas_call(
        _gate_sample_kernel,
        grid=(bsz // tb,),
        in_specs=[
            pl.BlockSpec((tb, t, w), lambda i: (i, 0, 0)),
            pl.BlockSpec((tb, t, w), lambda i: (i, 0, 0)),
            pl.BlockSpec((t, t, w), lambda i: (0, 0, 0)),
            pl.BlockSpec((t, w), lambda i: (0, 0)),
        ],
        out_specs=pl.BlockSpec((tb, t, w), lambda i: (i, 0, 0)),
        out_shape=jax.ShapeDtypeStruct((bsz, t, w), BF16),
        compiler_params=_params("parallel"),
        name="gate_sample",
    )(u, g, wt, bt)


def _attn_c_prompt_kernel(q_ref, kp_ref, kc_ref, vp_ref, vc_ref, bias_ref, o_ref, ob_ref, lb_ref, *, sb_len):
    sb = pl.program_id(0)

    def rows(ref, start, n, dil):
        return ref[pl.ds(start, n), :] if dil == 1 else ref[pl.ds(start, n, stride=dil), :]

    for bi, (window, dil) in enumerate(DILATED_BRANCHES):
        nstep = window // dil
        span = dil * nstep
        col = lax.broadcasted_iota(jnp.int32, (1, 2 * nstep), 1)
        no_prev = jnp.where(jnp.logical_and(sb == 0, col < nstep), np.float32(NEG_INF), np.float32(0.0))
        for blk in range(sb_len // span):
            base = blk * span
            for r in range(dil):
                q = rows(q_ref, base + r, nstep, dil).astype(BF16)
                if blk == 0:
                    k_prev = rows(kp_ref, sb_len - span + r, nstep, dil)
                    v_prev = rows(vp_ref, sb_len - span + r, nstep, dil)
                else:
                    k_prev = rows(kc_ref, base - span + r, nstep, dil)
                    v_prev = rows(vc_ref, base - span + r, nstep, dil)
                k = jnp.concatenate([k_prev, rows(kc_ref, base + r, nstep, dil)], axis=0).astype(BF16)
                v = jnp.concatenate([v_prev, rows(vc_ref, base + r, nstep, dil)], axis=0).astype(BF16)
                s = _dot_nt(q, k) + bias_ref[bi]
                if blk == 0:
                    s = s + no_prev
                m = jnp.max(s, axis=-1, keepdims=True)
                e = jnp.exp(s - m)
                den = jnp.sum(e, axis=-1, keepdims=True)
                o = _dot(e.astype(BF16), v) / den
                lse = jnp.broadcast_to(m + jnp.log(den), o.shape)
                if dil == 1:
                    ob_ref[bi, pl.ds(base, nstep), :] = o
                    lb_ref[bi, pl.ds(base, nstep), :] = lse
                else:
                    ob_ref[bi, pl.ds(base + r, nstep, stride=dil), :] = o
                    lb_ref[bi, pl.ds(base + r, nstep, stride=dil), :] = lse
    ls = [lb_ref[b] for b in range(len(DILATED_BRANCHES))]
    mm = functools.reduce(jnp.maximum, ls)
    es = [jnp.exp(l - mm) for l in ls]
    den = functools.reduce(lambda a, b: a + b, es)
    out = functools.reduce(lambda a, b: a + b, [(e / den) * ob_ref[b] for b, e in enumerate(es)])
    o_ref[...] = out.astype(o_ref.dtype)


def _attn_c_prompt(zf, s_len, bias):
    d = zf.shape[1] // 3
    nbr = len(DILATED_BRANCHES)
    sb_len = max(window for window, _ in DILATED_BRANCHES)
    assert s_len % sb_len == 0 and all(sb_len % window == 0 for window, _ in DILATED_BRANCHES)
    nstep = bias.shape[2]
    prev = lambda part: (lambda i, h: (jnp.maximum(i - 1, 0), part * H_C + h))
    cur = lambda part: (lambda i, h: (i, part * H_C + h))
    return pl.pallas_call(
        functools.partial(_attn_c_prompt_kernel, sb_len=sb_len),
        grid=(s_len // sb_len, H_C),
        in_specs=[
            pl.BlockSpec((sb_len, D_C), cur(0)),
            pl.BlockSpec((sb_len, D_C), prev(1)),
            pl.BlockSpec((sb_len, D_C), cur(1)),
            pl.BlockSpec((sb_len, D_C), prev(2)),
            pl.BlockSpec((sb_len, D_C), cur(2)),
            pl.BlockSpec((nbr, None, nstep, 2 * nstep), lambda i, h: (0, h, 0, 0)),
        ],
        out_specs=pl.BlockSpec((sb_len, D_C), lambda i, h: (i, h)),
        out_shape=jax.ShapeDtypeStruct((s_len, d), BF16),
        scratch_shapes=[pltpu.VMEM((nbr, sb_len, D_C), F32), pltpu.VMEM((nbr, sb_len, D_C), F32)],
        compiler_params=_params("parallel", "arbitrary"),
        name="attn_c_prompt",
    )(zf, zf, zf, zf, zf, bias)


def _bias_tiles_c_prompt(rel_bias):
    idx = []
    for window, dil in DILATED_BRANCHES:
        nstep = window // dil
        step = nstep + np.arange(nstep)[:, None] - np.arange(2 * nstep)[None, :]
        band = (step >= 0) & (step <= nstep)
        idx.append(np.where(band, _bucket_np(np.clip(step, 0, nstep) * dil), MASKED))
    nstep = idx[0].shape[0]
    assert all(i.shape == (nstep, 2 * nstep) for i in idx)
    nbr = len(idx)
    idx = np.stack(idx).reshape(1, nbr * nstep, 2 * nstep).astype(np.int32)
    tab = jnp.stack([_head_table(rel_bias, np.full(2 * nstep, h)) for h in range(H_C)])
    out = _bias_expand(idx, tab, tr=nstep)
    return jnp.swapaxes(out.reshape(H_C, nbr, nstep, 2 * nstep), 0, 1)


def _attn_c_sample_kernel(wq_ref, kn_ref, vn_ref, bp_ref, bn_ref, kh_ref, kf_ref, vh_ref, vf_ref, o_ref,
                          s_ref, p_ref, pn_ref, acc_ref, *, nck, n_half, ck):
    c = pl.program_id(1)
    t_new = kn_ref.shape[0]
    hk = ck // 2

    def heads_to_lanes(ref, n):
        return jnp.concatenate([ref[pl.ds(h, n, stride=H_C), :] for h in range(H_C)], axis=1).astype(BF16)

    def half_to_lanes(ref):
        kept = KEY_GROUP // 2
        return jnp.concatenate(
            [jnp.concatenate([ref[g, pl.ds(h, kept, stride=H_C), :] for g in range(ref.shape[0])], axis=0)
             for h in range(H_C)], axis=1).astype(BF16)

    def half_rows(cc):
        return pl.ds(pl.multiple_of(cc * hk, hk), hk)

    def full_rows(cc):
        return pl.ds(pl.multiple_of(n_half * hk + (cc - n_half) * ck, hk), ck)

    @pl.when(c < n_half)
    def _():
        s_ref[half_rows(c), :] = _dot(half_to_lanes(kh_ref), wq_ref[...])

    @pl.when(jnp.logical_and(c >= n_half, c < nck))
    def _():
        s_ref[full_rows(c), :] = _dot(heads_to_lanes(kf_ref, ck), wq_ref[...])

    @pl.when(c == nck - 1)
    def _():
        s_past = s_ref[...]
        s_new = _dot(kn_ref[...], wq_ref[...])
        es, lses = [], []
        for b in range(bp_ref.shape[0]):
            sp = s_past + bp_ref[b]
            sn = s_new + bn_ref[b]
            m = jnp.maximum(jnp.max(sp, axis=0, keepdims=True), jnp.max(sn, axis=0, keepdims=True))
            ep = jnp.exp(sp - m)
            en = jnp.exp(sn - m)
            den = jnp.sum(ep, axis=0, keepdims=True) + jnp.sum(en, axis=0, keepdims=True)
            es.append((ep, en, den))
            lses.append(m + jnp.log(den))
        mm = functools.reduce(jnp.maximum, lses)
        ws = [jnp.exp(l - mm) for l in lses]
        wsum = functools.reduce(lambda a, b: a + b, ws)
        pp = jnp.zeros(s_past.shape, F32)
        pn = jnp.zeros(s_new.shape, F32)
        for (ep, en, den), w in zip(es, ws):
            coef = (w / wsum) / den
            pp = pp + coef * ep
            pn = pn + coef * en
        p_ref[...] = pp.astype(BF16)
        pn_ref[...] = pn.astype(BF16)

    @pl.when(c == nck)
    def _():
        acc_ref[...] = _dot_tn(pn_ref[...], vn_ref[...])

    @pl.when(jnp.logical_and(c >= nck, c < nck + n_half))
    def _():
        acc_ref[...] += _dot_tn(p_ref[half_rows(c - nck), :], half_to_lanes(vh_ref))

    @pl.when(c >= nck + n_half)
    def _():
        acc_ref[...] += _dot_tn(p_ref[full_rows(c - nck), :], heads_to_lanes(vf_ref, ck))

    @pl.when(c == 2 * nck - 1)
    def _():
        for h in range(H_C):
            o_ref[:, h * D_C:(h + 1) * D_C] = acc_ref[h * t_new:(h + 1) * t_new, h * D_C:(h + 1) * D_C].astype(o_ref.dtype)


KEY_GROUP = max(dil for _, dil in DILATED_BRANCHES)
CK_C = 512


def _c_sample_plan(wlen, t, ck):
    dist = wlen + np.arange(t)[None, :] - np.arange(wlen)[:, None]
    need = np.zeros(wlen, bool)
    for window, dil in DILATED_BRANCHES:
        need |= ((dist % dil == 0) & (dist <= window)).any(axis=1)
    nck = wlen // ck
    first_half = (np.arange(wlen) % KEY_GROUP) < KEY_GROUP // 2
    n_half = 0
    while n_half < nck and not (need & ~first_half)[n_half * ck:(n_half + 1) * ck].any():
        n_half += 1
    keep = first_half | (np.arange(wlen) >= n_half * ck)
    return n_half, np.nonzero(keep)[0]


def _attn_c_sample(zb_s, cache_k, cache_v, o_idx, bias_past, bias_new, *, ck=CK_C):
    bsz, t, d3 = zb_s.shape
    d = d3 // 3
    wlen = cache_k.shape[2]
    nck = wlen // ck
    nbr = bias_past.shape[0]
    n_half, kidx = _c_sample_plan(wlen, t, ck)
    nrows = len(kidx)
    assert H_C * t == LANE and wlen % ck == 0 and ck % (2 * KEY_GROUP) == 0
    assert 1 <= n_half < nck and bias_past.shape[1] == nrows
    qt = jnp.swapaxes(zb_s[:, :, :d], 1, 2)
    rows = np.arange(d)[:, None] // D_C
    cols = np.arange(LANE)[None, :] // t
    wq = jnp.where(jnp.asarray(rows == cols), jnp.tile(qt, (1, 1, H_C)), jnp.zeros((), BF16))
    kn = zb_s[:, :, d:2 * d]
    vn = zb_s[:, :, 2 * d:]
    grp = KEY_GROUP * H_C
    full = lambda a: a.reshape(a.shape[0], bsz, wlen * H_C, D_C)
    half = lambda a: a.reshape(a.shape[0], bsz, wlen // KEY_GROUP, grp, D_C)
    prev = lambda b: jnp.maximum(b - 1, 0)
    kh_map = lambda b, c: (o_idx, b, jnp.minimum(c, n_half - 1), 0, 0)
    kf_map = lambda b, c: (o_idx, jnp.where(c >= n_half, b, prev(b)), jnp.where(c >= n_half, jnp.minimum(c, nck - 1), nck - 1), 0)
    vh_map = lambda b, c: (o_idx, jnp.where(c >= nck, b, prev(b)), jnp.where(c >= nck, jnp.minimum(c - nck, n_half - 1), n_half - 1), 0, 0)
    vf_map = lambda b, c: (o_idx, jnp.where(c >= nck + n_half, b, prev(b)), jnp.where(c >= nck + n_half, c - nck, nck - 1), 0)
    half_spec = lambda m: pl.BlockSpec((None, None, ck // KEY_GROUP, grp // 2, D_C), m)
    full_spec = lambda m: pl.BlockSpec((None, None, ck * H_C, D_C), m)
    return pl.pallas_call(
        functools.partial(_attn_c_sample_kernel, nck=nck, n_half=n_half, ck=ck),
        grid=(bsz, 2 * nck),
        in_specs=[
            pl.BlockSpec((None, d, LANE), lambda b, c: (b, 0, 0)),
            pl.BlockSpec((None, t, d), lambda b, c: (b, 0, 0)),
            pl.BlockSpec((None, t, d), lambda b, c: (b, 0, 0)),
            pl.BlockSpec((nbr, nrows, LANE), lambda b, c: (0, 0, 0)),
            pl.BlockSpec((nbr, t, LANE), lambda b, c: (0, 0, 0)),
            half_spec(kh_map), full_spec(kf_map), half_spec(vh_map), full_spec(vf_map),
        ],
        out_specs=pl.BlockSpec((None, t, d), lambda b, c: (b, 0, 0)),
        out_shape=jax.ShapeDtypeStruct((bsz, t, d), BF16),
        scratch_shapes=[pltpu.VMEM((nrows, LANE), F32), pltpu.VMEM((nrows, LANE), BF16),
                        pltpu.VMEM((t, LANE), BF16), pltpu.VMEM((LANE, d), F32)],
        compiler_params=_params("arbitrary", "arbitrary"),
        name="attn_c_sample",
    )(wq, kn, vn, bias_past, bias_new, half(cache_k), full(cache_k), half(cache_v), full(cache_v))


def _bias_c_sample(rel_bias, wlen, t):
    _, kidx = _c_sample_plan(wlen, t, CK_C)
    nrows = len(kidx)
    krow = np.concatenate([kidx, wlen + np.arange(t)])[:, None]
    col = np.arange(LANE)[None, :]
    dist = wlen + (col % t) - krow
    idx = []
    for window, dil in DILATED_BRANCHES:
        ok = (dist >= 0) & (dist % dil == 0) & (dist <= window)
        idx.append(np.where(ok, _bucket_np(dist), MASKED))
    idx = np.stack(idx).astype(np.int32)
    pad = (-idx.shape[1]) % 8
    idx = np.pad(idx, ((0, 0), (0, pad), (0, 0)), constant_values=MASKED)
    tab = jnp.broadcast_to(_head_table(rel_bias, np.arange(LANE) // t)[None], (idx.shape[0], N_BUCKETS + 1, LANE))
    out = _bias_expand(idx, tab, tr=idx.shape[1])
    return out[:, :nrows], out[:, nrows:nrows + t]


def kernel(x_prompt, x_sample, cache_k_a, cache_v_a, page_table, cache_k_c, cache_v_c, rel_bias, norm_gains, w_ffn_gate, w_ffn_up, w_ffn_down, w_in_ab, w_out_ab, lambda_qk, subln_gain, ln_v_gain, ln_v_bias, w_spatial, b_spatial, w_in_c, w_out_c):
    bp, s_len, d = x_prompt.shape
    bs, t_new, _ = x_sample.shape
    assert bp == 1
    n_s = bs * t_new
    depth = norm_gains.shape[0]
    xp = x_prompt.reshape(s_len, d)
    xs = x_sample.reshape(n_s, d)

    tq_a = 512
    past_a = page_table.shape[1] * cache_k_a.shape[2]
    bias_a_prompt = _bias_tiles_a_prompt(rel_bias, tq_a)
    bias_a_past, bias_a_new = _bias_a_sample(rel_bias, past_a, t_new)
    bias_c_prompt = _bias_tiles_c_prompt(rel_bias)
    bias_c_past, bias_c_new = _bias_c_sample(rel_bias, cache_k_c.shape[2], t_new)

    ka_p, va_p, ka_s, va_s, vb_p, vb_s = [], [], [], [], [], []
    kc_p, vc_p, kc_s, vc_s = [], [], [], []
    for li in range(depth):
        ng = norm_gains[li]

        def half_ffn(xp, xs, j):
            wts = (ng[2 * j, 0], ng[2 * j, 1], w_ffn_gate[li, j].astype(BF16), w_ffn_up[li, j].astype(BF16),
                   w_ffn_down[li, j].astype(BF16))
            return _ffn_half(xp, *wts), _ffn_half(xs, *wts)

        xp, xs = half_ffn(xp, xs, 0)
        if li % 2 == 0:
            e = li // 2
            w = H_A * 2 * D_QK
            proj = (ng[1, 0], w_in_ab[e].astype(BF16), ln_v_gain[e], ln_v_bias[e])
            k_p, v_p, u_p, g_p, zb_p = _proj_ab(xp, *proj)
            k_s, v_s, u_s, g_s, zb_s = _proj_ab(xs, *proj)
            ka_p.append(k_p.reshape(1, s_len, H_A, 2 * D_QK))
            va_p.append(v_p.reshape(1, s_len, H_A, 2 * D_QK))
            ka_s.append(k_s.reshape(bs, t_new, H_A, 2 * D_QK))
            va_s.append(v_s.reshape(bs, t_new, H_A, 2 * D_QK))
            last = s_len - ((s_len - 1) // CHUNK_B) * CHUNK_B
            vb_p.append(g_p[s_len - last:].reshape(1, last, w))
            vb_s.append(g_s.reshape(bs, t_new, w))

            o_p = _attn_a_prompt(zb_p, bias_a_prompt, lambda_qk[e], subln_gain[e], li, tq=tq_a)
            o_s = _attn_a_sample(zb_s.reshape(bs, t_new, 3 * w), cache_k_a, cache_v_a, page_table, e,
                                 bias_a_past, bias_a_new, lambda_qk[e], subln_gain[e], li)
            gate_p = _gate_prompt(u_p, g_p, w_spatial[e], b_spatial[e])
            gate_s = _gate_sample(u_s.reshape(bs, t_new, w), g_s.reshape(bs, t_new, w), w_spatial[e], b_spatial[e])
            w_out = w_out_ab[e].astype(BF16)
            xp = _out_proj(o_p, gate_p, w_out, xp, ng[1, 1])
            xs = _out_proj(o_s.reshape(n_s, w), gate_s.reshape(n_s, w), w_out, xs, ng[1, 1])
        else:
            o = li // 2
            w_in = w_in_c[o].astype(BF16)
            zf_p = _proj_c(xp, ng[1, 0], w_in)
            zf_s = _proj_c(xs, ng[1, 0], w_in)
            keep = min(max(wd for wd, _ in DILATED_BRANCHES), s_len)
            kc_p.append(zf_p[s_len - keep:, d:2 * d].reshape(1, keep, H_C, D_C))
            vc_p.append(zf_p[s_len - keep:, 2 * d:].reshape(1, keep, H_C, D_C))
            kc_s.append(zf_s[:, d:2 * d].reshape(bs, t_new, H_C, D_C))
            vc_s.append(zf_s[:, 2 * d:].reshape(bs, t_new, H_C, D_C))
            o_p = _attn_c_prompt(zf_p, s_len, bias_c_prompt)
            o_s = _attn_c_sample(zf_s.astype(BF16).reshape(bs, t_new, 3 * d), cache_k_c, cache_v_c, o,
                                 bias_c_past, bias_c_new).reshape(n_s, d)
            w_out = w_out_c[o].astype(BF16)
            xp = _out_proj(o_p, o_p, w_out, xp, ng[1, 1], a_blk=0, b_blk=1)
            xs = _out_proj(o_s, o_s, w_out, xs, ng[1, 1], a_blk=0, b_blk=1)
        xp, xs = half_ffn(xp, xs, 1)

    y_prompt = xp.reshape(1, s_len, d)
    y_sample = xs.reshape(bs, t_new, d)
    return (y_prompt, y_sample,
            jnp.stack(ka_p, axis=1), jnp.stack(va_p, axis=1), jnp.stack(ka_s, axis=1), jnp.stack(va_s, axis=1),
            jnp.stack(vb_p, axis=0), jnp.stack(vb_s, axis=0),
            jnp.stack(kc_p, axis=0), jnp.stack(vc_p, axis=0), jnp.stack(kc_s, axis=0), jnp.stack(vc_s, axis=0))
```

```python
import functools
import math

import numpy as np
import jax
import jax.numpy as jnp
from jax import lax
from jax.experimental import pallas as pl
from jax.experimental.pallas import tpu as pltpu

F32 = jnp.float32
BF16 = jnp.bfloat16

EPS = 1e-6
NEG_INF = -1e30
LOG2E = math.log2(math.e)
LANE = 128
VMEM_LIMIT = 56 * 1024 * 1024

H_A = 8
D_QK = 64
G_B = 8
C_B = 128
CHUNK_B = 128
H_C = 16
D_C = 128
DILATED_BRANCHES = ((128, 1), (512, 4), (2048, 16))
N_BUCKETS = 32
MAX_DISTANCE = 128
MASKED = N_BUCKETS


def _params(*sem):
    return pltpu.CompilerParams(dimension_semantics=sem, vmem_limit_bytes=VMEM_LIMIT)


def _rms(x, g):
    return x * lax.rsqrt(jnp.mean(x * x, axis=-1, keepdims=True) + EPS) * g


def _gelu(x):
    return 0.5 * x * (1.0 + lax.erf(x * np.float32(math.sqrt(0.5))))


def _dot(a, b):
    return jnp.dot(a, b, preferred_element_type=F32)


def _dot_nt(a, b):
    return lax.dot_general(a, b, (((1,), (1,)), ((), ())), preferred_element_type=F32)


def _dot_tn(a, b):
    return lax.dot_general(a, b, (((0,), (0,)), ((), ())), preferred_element_type=F32)


def _lambda_init(layer):
    return 0.8 - 0.6 * math.exp(-0.3 * layer)


def _diff_lambda(lam_ref, layer):
    lp = lam_ref[...]
    a = jnp.sum(lp[0:1] * lp[1:2], axis=-1, keepdims=True)
    b = jnp.sum(lp[2:3] * lp[3:4], axis=-1, keepdims=True)
    return jnp.exp(a) - jnp.exp(b) + np.float32(_lambda_init(layer))


def _ffn_kernel(x_ref, gpre_ref, gpost_ref, wg_ref, wu_ref, wd_ref, o_ref, h_ref):
    f = pl.program_id(1)

    @pl.when(f == 0)
    def _():
        h_ref[...] = _rms(x_ref[...], gpre_ref[...]).astype(BF16)
        o_ref[...] = jnp.zeros_like(o_ref)

    h = h_ref[...]
    a = _dot(h, wg_ref[...])
    b = _dot(h, wu_ref[...])
    s = (a * jax.nn.sigmoid(a)) * b
    o_ref[...] += _dot(s.astype(BF16), wd_ref[...])

    @pl.when(f == pl.num_programs(1) - 1)
    def _():
        o_ref[...] = x_ref[...] + 0.5 * _rms(o_ref[...], gpost_ref[...])


def _ffn_half(x, g_pre, g_post, wg, wu, wd, li, hj, *, tm=512, tf=512):
    n, d = x.shape
    f = wg.shape[-1]
    return pl.pallas_call(
        _ffn_kernel,
        grid=(n // tm, f // tf),
        in_specs=[
            pl.BlockSpec((tm, d), lambda i, j: (i, 0)),
            pl.BlockSpec((1, d), lambda i, j: (0, 0)),
            pl.BlockSpec((1, d), lambda i, j: (0, 0)),
            pl.BlockSpec((None, None, d, tf), lambda i, j: (li, hj, 0, j)),
            pl.BlockSpec((None, None, d, tf), lambda i, j: (li, hj, 0, j)),
            pl.BlockSpec((None, None, tf, d), lambda i, j: (li, hj, j, 0)),
        ],
        out_specs=pl.BlockSpec((tm, d), lambda i, j: (i, 0)),
        out_shape=jax.ShapeDtypeStruct((n, d), F32),
        scratch_shapes=[pltpu.VMEM((tm, d), BF16)],
        compiler_params=_params("parallel", "arbitrary"),
        name="ffn_half",
    )(x, g_pre.reshape(1, d), g_post.reshape(1, d), wg, wu, wd)


def _proj_ab_kernel(x_ref, g_ref, w_ref, lng_ref, lnb_ref, k_ref, v_ref, u_ref, gv_ref, zb_ref, h_ref):
    j = pl.program_id(1)

    @pl.when(j == 0)
    def _():
        h_ref[...] = _rms(x_ref[...], g_ref[...]).astype(BF16)

    z = _dot(h_ref[...], w_ref[...])

    @pl.when(j == 0)
    def _():
        zb_ref[...] = (z * np.float32(D_QK ** -0.5 * LOG2E)).astype(BF16)

    @pl.when(j == 1)
    def _():
        k_ref[...] = z
        zb_ref[...] = z.astype(BF16)

    @pl.when(j == 2)
    def _():
        v_ref[...] = z
        zb_ref[...] = z.astype(BF16)

    @pl.when(j == 3)
    def _():
        u_ref[...] = _gelu(z)

    @pl.when(j == 4)
    def _():
        a = _gelu(z)
        c = a - jnp.mean(a, axis=-1, keepdims=True)
        y = c * lax.rsqrt(jnp.mean(c * c, axis=-1, keepdims=True) + EPS)
        gv_ref[...] = y * lng_ref[...] + lnb_ref[...]


def _proj_ab(x, g_pre, w_in, ln_g, ln_b, *, tm=512):
    n, d = x.shape
    w = w_in.shape[1] // 5
    row = pl.BlockSpec((tm, w), lambda i, j: (i, 0))
    return pl.pallas_call(
        _proj_ab_kernel,
        grid=(n // tm, 5),
        in_specs=[
            pl.BlockSpec((tm, d), lambda i, j: (i, 0)),
            pl.BlockSpec((1, d), lambda i, j: (0, 0)),
            pl.BlockSpec((d, w), lambda i, j: (0, j)),
            pl.BlockSpec((1, w), lambda i, j: (0, 0)),
            pl.BlockSpec((1, w), lambda i, j: (0, 0)),
        ],
        out_specs=[row, row, row, row, pl.BlockSpec((tm, w), lambda i, j: (i, jnp.minimum(j, 2)))],
        out_shape=[jax.ShapeDtypeStruct((n, w), F32)] * 4 + [jax.ShapeDtypeStruct((n, 3 * w), BF16)],
        scratch_shapes=[pltpu.VMEM((tm, d), BF16)],
        compiler_params=_params("parallel", "arbitrary"),
        name="proj_ab",
    )(x, g_pre.reshape(1, d), w_in, ln_g.reshape(1, w), ln_b.reshape(1, w))


def _proj_c_kernel(x_ref, g_ref, w_ref, zf_ref, h_ref):
    j = pl.program_id(1)

    @pl.when(j == 0)
    def _():
        h_ref[...] = _rms(x_ref[...], g_ref[...]).astype(BF16)

    z = _dot(h_ref[...], w_ref[...])

    @pl.when(j == 0)
    def _():
        zf_ref[...] = z * np.float32(D_C ** -0.5)

    @pl.when(j > 0)
    def _():
        zf_ref[...] = z


def _proj_c(x, g_pre, w_in, *, tm=512):
    n, d = x.shape
    w = w_in.shape[1] // 3
    return pl.pallas_call(
        _proj_c_kernel,
        grid=(n // tm, 3),
        in_specs=[
            pl.BlockSpec((tm, d), lambda i, j: (i, 0)),
            pl.BlockSpec((1, d), lambda i, j: (0, 0)),
            pl.BlockSpec((d, w), lambda i, j: (0, j)),
        ],
        out_specs=pl.BlockSpec((tm, w), lambda i, j: (i, j)),
        out_shape=jax.ShapeDtypeStruct((n, 3 * w), F32),
        scratch_shapes=[pltpu.VMEM((tm, d), BF16)],
        compiler_params=_params("parallel", "arbitrary"),
        name="proj_c",
    )(x, g_pre.reshape(1, d), w_in)


def _out_proj_kernel(a_ref, b_ref, wa_ref, wb_ref, x_ref, g_ref, o_ref):
    y = _dot(a_ref[...].astype(BF16), wa_ref[...]) + _dot(b_ref[...].astype(BF16), wb_ref[...])
    o_ref[...] = x_ref[...] + _rms(y, g_ref[...])


def _out_proj(a, b, w_out, x, g_post, *, a_blk=0, b_blk=0, tm=512):
    n, d = x.shape
    kh = w_out.shape[0] // 2
    return pl.pallas_call(
        _out_proj_kernel,
        grid=(n // tm,),
        in_specs=[
            pl.BlockSpec((tm, kh), lambda i: (i, a_blk)),
            pl.BlockSpec((tm, kh), lambda i: (i, b_blk)),
            pl.BlockSpec((kh, d), lambda i: (0, 0)),
            pl.BlockSpec((kh, d), lambda i: (1, 0)),
            pl.BlockSpec((tm, d), lambda i: (i, 0)),
            pl.BlockSpec((1, d), lambda i: (0, 0)),
        ],
        out_specs=pl.BlockSpec((tm, d), lambda i: (i, 0)),
        out_shape=jax.ShapeDtypeStruct((n, d), F32),
        compiler_params=_params("parallel"),
        name="out_proj",
    )(a, b, w_out, w_out, x, g_post.reshape(1, d))


def _bucket_np(dist):
    max_exact = N_BUCKETS // 2
    d = np.maximum(dist, 0)
    ratio = np.log(np.maximum(d, 1).astype(np.float32) / np.float32(max_exact)) / np.float32(math.log(MAX_DISTANCE / max_exact))
    large = np.minimum(max_exact + (ratio * (N_BUCKETS - max_exact)).astype(np.int32), N_BUCKETS - 1)
    return np.where(d < max_exact, d, large).astype(np.int32)


def _bias_expand_kernel(idx_ref, tab_ref, o_ref):
    idx = idx_ref[...]
    val = jnp.zeros(idx.shape, F32)
    for b in range(N_BUCKETS + 1):
        val = jnp.where(idx == b, tab_ref[b:b + 1, :], val)
    o_ref[...] = val


def _bias_expand(idx, tab, *, tr=256):
    gi, r, c = idx.shape
    g = tab.shape[0]
    tr = min(tr, r)
    return pl.pallas_call(
        _bias_expand_kernel,
        grid=(g, r // tr),
        in_specs=[
            pl.BlockSpec((None, tr, c), (lambda a, i: (a, i, 0)) if gi == g else (lambda a, i: (0, i, 0))),
            pl.BlockSpec((None, N_BUCKETS + 1, c), lambda a, i: (a, 0, 0)),
        ],
        out_specs=pl.BlockSpec((None, tr, c), lambda a, i: (a, i, 0)),
        out_shape=jax.ShapeDtypeStruct((g, r, c), F32),
        compiler_params=_params("parallel", "parallel"),
        name="bias_expand",
    )(jnp.asarray(idx), tab)


def _head_table(rel_bias, heads_of_col):
    t = rel_bias.astype(F32)[:, np.asarray(heads_of_col)]
    return jnp.concatenate([t, jnp.full((1, t.shape[1]), NEG_INF, F32)], axis=0)


def _attn_a_prompt_kernel(q_ref, k_ref, v_ref, bias_ref, lam_ref, sg_ref, o_ref, m_ref, l_ref, acc_ref, *, tq, layer):
    qi = pl.program_id(1)
    q = q_ref[...]
    lane = lax.broadcasted_iota(jnp.int32, q.shape, 1)
    zero = jnp.zeros_like(q)
    qq = jnp.concatenate([jnp.where(lane < D_QK, q, zero), jnp.where(lane >= D_QK, q, zero)], axis=0)
    m_ref[...] = jnp.full(m_ref.shape, NEG_INF, F32)
    l_ref[...] = jnp.zeros(l_ref.shape, F32)
    acc_ref[...] = jnp.zeros(acc_ref.shape, F32)

    def tile(j, bias_tile):
        off = pl.multiple_of(j * tq, tq)
        s = _dot_nt(qq, k_ref[pl.ds(off, tq), :])
        if bias_tile is not None:
            b = bias_ref[bias_tile]
            s = s + jnp.concatenate([b, b], axis=0)
        st = [s[:, c * LANE:(c + 1) * LANE] for c in range(tq // LANE)]
        m_prev = m_ref[...]
        m_new = jnp.maximum(m_prev, jnp.max(functools.reduce(jnp.maximum, st), axis=-1, keepdims=True))
        alpha = jnp.exp2(m_prev - m_new)
        ps = [jnp.exp2(t - m_new) for t in st]
        l_ref[...] = alpha * l_ref[...] + functools.reduce(lambda a, b: a + b, ps)
        p = jnp.concatenate([x.astype(BF16) for x in ps], axis=1)
        acc_ref[...] = alpha * acc_ref[...] + _dot(p, v_ref[pl.ds(off, tq), :])
        m_ref[...] = m_new

    def far_body(j, carry):
        tile(j, None)
        return carry

    lax.fori_loop(0, qi - 1, far_body, 0)

    @pl.when(qi >= 1)
    def _():
        tile(qi - 1, 1)

    tile(qi, 0)
    lam = _diff_lambda(lam_ref, layer)
    o = acc_ref[...] / jnp.sum(l_ref[...], axis=-1, keepdims=True)
    o = o[:tq] - lam * o[tq:]
    o_ref[...] = (_rms(o, sg_ref[...]) * np.float32(1.0 - _lambda_init(layer))).astype(o_ref.dtype)


def _attn_a_prompt(zb, bias_tiles, lam_p, subln_g, layer, *, tq=512):
    s_len = zb.shape[0]
    w = zb.shape[1] // 3
    hw = w // H_A
    return pl.pallas_call(
        functools.partial(_attn_a_prompt_kernel, tq=tq, layer=layer),
        grid=(H_A, s_len // tq),
        in_specs=[
            pl.BlockSpec((tq, hw), lambda h, i: (i, h)),
            pl.BlockSpec((s_len, hw), lambda h, i: (0, H_A + h)),
            pl.BlockSpec((s_len, hw), lambda h, i: (0, 2 * H_A + h)),
            pl.BlockSpec((None, 2, tq, tq), lambda h, i: (h, 0, 0, 0)),
            pl.BlockSpec((4, D_QK), lambda h, i: (0, 0)),
            pl.BlockSpec((1, hw), lambda h, i: (0, 0)),
        ],
        out_specs=pl.BlockSpec((tq, hw), lambda h, i: (i, h)),
        out_shape=jax.ShapeDtypeStruct((s_len, w), BF16),
        scratch_shapes=[pltpu.VMEM((2 * tq, LANE), F32), pltpu.VMEM((2 * tq, LANE), F32), pltpu.VMEM((2 * tq, hw), F32)],
        compiler_params=_params("parallel", "arbitrary"),
        name="attn_a_prompt",
    )(zb, zb, zb, bias_tiles, lam_p, subln_g.reshape(1, hw))


def _bias_tiles_a_prompt(rel_bias, tq):
    assert _bucket_np(np.array([tq + 1]))[0] == N_BUCKETS - 1
    i = np.arange(tq)[:, None]
    j = np.arange(tq)[None, :]
    idx = []
    for delta in (0, tq):
        dist = delta + i - j
        idx.append(np.where(dist >= 0, _bucket_np(dist), MASKED))
    idx = np.stack(idx).reshape(1, 2 * tq, tq).astype(np.int32)
    shifted = (rel_bias - rel_bias[N_BUCKETS - 1:]) * np.float32(LOG2E)
    tab = jnp.stack([_head_table(shifted, np.full(tq, h)) for h in range(H_A)])
    return _bias_expand(idx, tab, tr=min(tq, 256)).reshape(H_A, 2, tq, tq)


def _attn_a_sample_kernel(pt_ref, wq_ref, kn_ref, vn_ref, bp_ref, bn_ref, lam_ref, sg_ref, *rest, n_pages, page, layer):
    k_pages = rest[:n_pages]
    v_pages = rest[n_pages:2 * n_pages]
    o_ref = rest[2 * n_pages]
    s_ref, p_ref = rest[2 * n_pages + 1:]
    wq = wq_ref[...]

    def heads_to_lanes(ref):
        return jnp.concatenate([ref[pl.ds(h, page, stride=H_A), :] for h in range(H_A)], axis=1).astype(BF16)

    for p in range(n_pages):
        s_ref[p * page:(p + 1) * page, :] = _dot(heads_to_lanes(k_pages[p]), wq) + bp_ref[p * page:(p + 1) * page, :]
    s_new = _dot(kn_ref[...], wq) + bn_ref[...]
    s_past = s_ref[...]
    m = jnp.maximum(jnp.max(s_past, axis=0, keepdims=True), jnp.max(s_new, axis=0, keepdims=True))
    e_past = jnp.exp2(s_past - m)
    e_new = jnp.exp2(s_new - m)
    den = jnp.sum(e_past, axis=0, keepdims=True) + jnp.sum(e_new, axis=0, keepdims=True)
    col = lax.broadcasted_iota(jnp.int32, den.shape, 1)
    t_new = kn_ref.shape[0]
    lam = _diff_lambda(lam_ref, layer)
    scale = jnp.where((col // t_new) % 2 == 0, 1.0, -lam) / den
    p_ref[...] = (e_past * scale).astype(BF16)
    acc = _dot_tn((e_new * scale).astype(BF16), vn_ref[...])
    for p in range(n_pages):
        acc = acc + _dot_tn(p_ref[p * page:(p + 1) * page, :], heads_to_lanes(v_pages[p]))
    for h in range(H_A):
        r0 = 2 * h * t_new
        o = acc[r0:r0 + t_new, h * LANE:(h + 1) * LANE] + acc[r0 + t_new:r0 + 2 * t_new, h * LANE:(h + 1) * LANE]
        o_ref[:, h * LANE:(h + 1) * LANE] = (_rms(o, sg_ref[...]) * np.float32(1.0 - _lambda_init(layer))).astype(o_ref.dtype)


def _attn_a_sample(zb_s, cache_k, cache_v, page_table, e, bias_past, bias_new, lam_p, subln_g, layer):
    bsz, t, w3 = zb_s.shape
    w = w3 // 3
    n_pages = page_table.shape[1]
    page = cache_k.shape[2]
    assert 2 * H_A * t == LANE
    qt = jnp.swapaxes(zb_s[:, :, :w], 1, 2)
    rows = np.arange(w)[:, None] // D_QK
    cols = np.arange(LANE)[None, :] // t
    wq = jnp.where(jnp.asarray(rows == cols), jnp.tile(qt, (1, 1, 2 * H_A)), jnp.zeros((), BF16))
    kn = zb_s[:, :, w:2 * w]
    vn = zb_s[:, :, 2 * w:]

    cache_k = cache_k.reshape(cache_k.shape[0], cache_k.shape[1], page * H_A, LANE)
    cache_v = cache_v.reshape(cache_v.shape[0], cache_v.shape[1], page * H_A, LANE)

    def page_spec(p):
        return pl.BlockSpec((None, None, page * H_A, LANE), lambda b, pt: (pt[b * n_pages + p], e, 0, 0))

    grid_spec = pltpu.PrefetchScalarGridSpec(
        num_scalar_prefetch=1,
        grid=(bsz,),
        in_specs=[
            pl.BlockSpec((None, w, LANE), lambda b, pt: (b, 0, 0)),
            pl.BlockSpec((None, t, w), lambda b, pt: (b, 0, 0)),
            pl.BlockSpec((None, t, w), lambda b, pt: (b, 0, 0)),
            pl.BlockSpec((n_pages * page, LANE), lambda b, pt: (0, 0)),
            pl.BlockSpec((t, LANE), lambda b, pt: (0, 0)),
            pl.BlockSpec((4, D_QK), lambda b, pt: (0, 0)),
            pl.BlockSpec((1, LANE), lambda b, pt: (0, 0)),
        ] + [page_spec(p) for p in range(n_pages)] + [page_spec(p) for p in range(n_pages)],
        out_specs=pl.BlockSpec((None, t, w), lambda b, pt: (b, 0, 0)),
        scratch_shapes=[pltpu.VMEM((n_pages * page, LANE), F32), pltpu.VMEM((n_pages * page, LANE), BF16)],
    )
    return pl.pallas_call(
        functools.partial(_attn_a_sample_kernel, n_pages=n_pages, page=page, layer=layer),
        grid_spec=grid_spec,
        out_shape=jax.ShapeDtypeStruct((bsz, t, w), BF16),
        compiler_params=_params("arbitrary"),
        name="attn_a_sample",
    )(page_table.reshape(-1), wq, kn, vn, bias_past, bias_new, lam_p, subln_g.reshape(1, LANE),
      *([cache_k] * n_pages), *([cache_v] * n_pages))


def _bias_a_sample(rel_bias, past, t):
    kpos = np.arange(past + t)[:, None]
    col = np.arange(LANE)[None, :]
    dist = past + (col % t) - kpos
    idx = np.where(dist >= 0, _bucket_np(dist), MASKED).astype(np.int32)
    pad = (-idx.shape[0]) % 8
    idx = np.pad(idx, ((0, pad), (0, 0)), constant_values=MASKED)[None]
    tab = _head_table(rel_bias * np.float32(LOG2E), np.arange(LANE) // (2 * t))[None]
    out = _bias_expand(idx, tab, tr=idx.shape[1])[0]
    return out[:past], out[past:past + t]


def _gate_prompt_kernel(u_ref, g_ref, w_ref, b_ref, o_ref, *, chunks):
    row = lax.broadcasted_iota(jnp.int32, (CHUNK_B, CHUNK_B), 0)
    col = lax.broadcasted_iota(jnp.int32, (CHUNK_B, CHUNK_B), 1)
    for gi in range(G_B):
        wg = jnp.where(row >= col, w_ref[gi], 0.0).astype(BF16)
        bg = b_ref[:, gi:gi + 1]
        for c in range(chunks):
            rs = slice(c * CHUNK_B, (c + 1) * CHUNK_B)
            cs = slice(gi * C_B, (gi + 1) * C_B)
            mixed = _dot(wg, g_ref[rs, cs].astype(BF16)) + bg
            o_ref[rs, cs] = (u_ref[rs, cs] * mixed).astype(o_ref.dtype)


def _gate_prompt(u, g, w_s, b_s, *, chunks=4):
    s_len, w = u.shape
    tm = chunks * CHUNK_B
    return pl.pallas_call(
        functools.partial(_gate_prompt_kernel, chunks=chunks),
        grid=(s_len // tm,),
        in_specs=[
            pl.BlockSpec((tm, w), lambda i: (i, 0)),
            pl.BlockSpec((tm, w), lambda i: (i, 0)),
            pl.BlockSpec((G_B, CHUNK_B, CHUNK_B), lambda i: (0, 0, 0)),
            pl.BlockSpec((CHUNK_B, G_B), lambda i: (0, 0)),
        ],
        out_specs=pl.BlockSpec((tm, w), lambda i: (i, 0)),
        out_shape=jax.ShapeDtypeStruct((s_len, w), BF16),
        compiler_params=_params("parallel"),
        name="gate_prompt",
    )(u, g, w_s, b_s.T)


def _gate_sample_kernel(u_ref, g_ref, w_ref, b_ref, o_ref):
    t = u_ref.shape[1]
    g = g_ref[...]
    mixed = jnp.zeros(g.shape, F32) + b_ref[...][None]
    for s in range(t):
        mixed = mixed + w_ref[:, s, :][None] * g[:, s:s + 1, :]
    o_ref[...] = (u_ref[...] * mixed).astype(o_ref.dtype)


def _gate_sample(u, g, w_s, b_s, *, tb=32):
    bsz, t, w = u.shape
    causal = np.tril(np.ones((t, t), bool))
    wt = jnp.where(jnp.asarray(causal)[None], w_s[:, :t, :t], 0.0)
    wt = jnp.repeat(jnp.transpose(wt, (1, 2, 0)), C_B, axis=2)
    bt = jnp.repeat(b_s[:, :t].T, C_B, axis=1)
    return pl.pallas_call(
        _gate_sample_kernel,
        grid=(bsz // tb,),
        in_specs=[
            pl.BlockSpec((tb, t, w), lambda i: (i, 0, 0)),
            pl.BlockSpec((tb, t, w), lambda i: (i, 0, 0)),
            pl.BlockSpec((t, t, w), lambda i: (0, 0, 0)),
            pl.BlockSpec((t, w), lambda i: (0, 0)),
        ],
        out_specs=pl.BlockSpec((tb, t, w), lambda i: (i, 0, 0)),
        out_shape=jax.ShapeDtypeStruct((bsz, t, w), BF16),
        compiler_params=_params("parallel"),
        name="gate_sample",
    )(u, g, wt, bt)


def _attn_c_prompt_kernel(q_ref, kp_ref, kc_ref, vp_ref, vc_ref, bias_ref, o_ref, ob_ref, lb_ref, *, sb_len):
    sb = pl.program_id(0)

    def rows(ref, start, n, dil):
        return ref[pl.ds(start, n), :] if dil == 1 else ref[pl.ds(start, n, stride=dil), :]

    for bi, (window, dil) in enumerate(DILATED_BRANCHES):
        nstep = window // dil
        span = dil * nstep
        col = lax.broadcasted_iota(jnp.int32, (1, 2 * nstep), 1)
        no_prev = jnp.where(jnp.logical_and(sb == 0, col < nstep), np.float32(NEG_INF), np.float32(0.0))
        for blk in range(sb_len // span):
            base = blk * span
            for r in range(dil):
                q = rows(q_ref, base + r, nstep, dil).astype(BF16)
                if blk == 0:
                    k_prev = rows(kp_ref, sb_len - span + r, nstep, dil)
                    v_prev = rows(vp_ref, sb_len - span + r, nstep, dil)
                else:
                    k_prev = rows(kc_ref, base - span + r, nstep, dil)
                    v_prev = rows(vc_ref, base - span + r, nstep, dil)
                k = jnp.concatenate([k_prev, rows(kc_ref, base + r, nstep, dil)], axis=0).astype(BF16)
                v = jnp.concatenate([v_prev, rows(vc_ref, base + r, nstep, dil)], axis=0).astype(BF16)
                s = _dot_nt(q, k) + bias_ref[bi]
                if blk == 0:
                    s = s + no_prev
                m = jnp.max(s, axis=-1, keepdims=True)
                e = jnp.exp(s - m)
                den = jnp.sum(e, axis=-1, keepdims=True)
                o = _dot(e.astype(BF16), v) / den
                lse = jnp.broadcast_to(m + jnp.log(den), o.shape)
                if dil == 1:
                    ob_ref[bi, pl.ds(base, nstep), :] = o
                    lb_ref[bi, pl.ds(base, nstep), :] = lse
                else:
                    ob_ref[bi, pl.ds(base + r, nstep, stride=dil), :] = o
                    lb_ref[bi, pl.ds(base + r, nstep, stride=dil), :] = lse
    ls = [lb_ref[b] for b in range(len(DILATED_BRANCHES))]
    mm = functools.reduce(jnp.maximum, ls)
    es = [jnp.exp(l - mm) for l in ls]
    den = functools.reduce(lambda a, b: a + b, es)
    out = functools.reduce(lambda a, b: a + b, [(e / den) * ob_ref[b] for b, e in enumerate(es)])
    o_ref[...] = out.astype(o_ref.dtype)


def _attn_c_prompt(zf, s_len, bias):
    d = zf.shape[1] // 3
    nbr = len(DILATED_BRANCHES)
    sb_len = max(window for window, _ in DILATED_BRANCHES)
    assert s_len % sb_len == 0 and all(sb_len % window == 0 for window, _ in DILATED_BRANCHES)
    nstep = bias.shape[2]
    prev = lambda part: (lambda i, h: (jnp.maximum(i - 1, 0), part * H_C + h))
    cur = lambda part: (lambda i, h: (i, part * H_C + h))
    return pl.pallas_call(
        functools.partial(_attn_c_prompt_kernel, sb_len=sb_len),
        grid=(s_len // sb_len, H_C),
        in_specs=[
            pl.BlockSpec((sb_len, D_C), cur(0)),
            pl.BlockSpec((sb_len, D_C), prev(1)),
            pl.BlockSpec((sb_len, D_C), cur(1)),
            pl.BlockSpec((sb_len, D_C), prev(2)),
            pl.BlockSpec((sb_len, D_C), cur(2)),
            pl.BlockSpec((nbr, None, nstep, 2 * nstep), lambda i, h: (0, h, 0, 0)),
        ],
        out_specs=pl.BlockSpec((sb_len, D_C), lambda i, h: (i, h)),
        out_shape=jax.ShapeDtypeStruct((s_len, d), BF16),
        scratch_shapes=[pltpu.VMEM((nbr, sb_len, D_C), F32), pltpu.VMEM((nbr, sb_len, D_C), F32)],
        compiler_params=_params("parallel", "arbitrary"),
        name="attn_c_prompt",
    )(zf, zf, zf, zf, zf, bias)


def _bias_tiles_c_prompt(rel_bias):
    idx = []
    for window, dil in DILATED_BRANCHES:
        nstep = window // dil
        step = nstep + np.arange(nstep)[:, None] - np.arange(2 * nstep)[None, :]
        band = (step >= 0) & (step <= nstep)
        idx.append(np.where(band, _bucket_np(np.clip(step, 0, nstep) * dil), MASKED))
    nstep = idx[0].shape[0]
    assert all(i.shape == (nstep, 2 * nstep) for i in idx)
    nbr = len(idx)
    idx = np.stack(idx).reshape(1, nbr * nstep, 2 * nstep).astype(np.int32)
    tab = jnp.stack([_head_table(rel_bias, np.full(2 * nstep, h)) for h in range(H_C)])
    out = _bias_expand(idx, tab, tr=nstep)
    return jnp.swapaxes(out.reshape(H_C, nbr, nstep, 2 * nstep), 0, 1)


def _attn_c_sample_kernel(wq_ref, kn_ref, vn_ref, bp_ref, bn_ref, kh_ref, kf_ref, vh_ref, vf_ref, o_ref,
                          s_ref, p_ref, *, n_half, n_full, ck):
    t_new = kn_ref.shape[0]
    hk = ck // 2
    kept = KEY_GROUP // 2
    gpc = ck // KEY_GROUP

    def full_to_lanes(ref, c):
        return jnp.concatenate([ref[pl.ds(c * ck * H_C + h, ck, stride=H_C), :] for h in range(H_C)],
                               axis=1).astype(BF16)

    def half_to_lanes(ref, c):
        return jnp.concatenate(
            [jnp.concatenate([ref[g, pl.ds(h, kept, stride=H_C), :] for g in range(c * gpc, (c + 1) * gpc)], axis=0)
             for h in range(H_C)], axis=1).astype(BF16)

    chunks = [(half_to_lanes, c, slice(c * hk, (c + 1) * hk)) for c in range(n_half)]
    chunks += [(full_to_lanes, c, slice(n_half * hk + c * ck, n_half * hk + (c + 1) * ck)) for c in range(n_full)]

    wq = wq_ref[...]
    for to_lanes, c, rs in chunks:
        ref = kh_ref if to_lanes is half_to_lanes else kf_ref
        s_ref[rs, :] = _dot(to_lanes(ref, c), wq)
    s_past = s_ref[...]
    s_new = _dot(kn_ref[...], wq)
    es, lses = [], []
    for b in range(bp_ref.shape[0]):
        sp = s_past + bp_ref[b]
        sn = s_new + bn_ref[b]
        m = jnp.maximum(jnp.max(sp, axis=0, keepdims=True), jnp.max(sn, axis=0, keepdims=True))
        ep = jnp.exp(sp - m)
        en = jnp.exp(sn - m)
        den = jnp.sum(ep, axis=0, keepdims=True) + jnp.sum(en, axis=0, keepdims=True)
        es.append((ep, en, den))
        lses.append(m + jnp.log(den))
    mm = functools.reduce(jnp.maximum, lses)
    ws = [jnp.exp(l - mm) for l in lses]
    wsum = functools.reduce(lambda a, b: a + b, ws)
    pp = jnp.zeros(s_past.shape, F32)
    pn = jnp.zeros(s_new.shape, F32)
    for (ep, en, den), w in zip(es, ws):
        coef = (w / wsum) / den
        pp = pp + coef * ep
        pn = pn + coef * en
    p_ref[...] = pp.astype(BF16)
    acc = _dot_tn(pn.astype(BF16), vn_ref[...])
    for to_lanes, c, rs in chunks:
        ref = vh_ref if to_lanes is half_to_lanes else vf_ref
        acc = acc + _dot_tn(p_ref[rs, :], to_lanes(ref, c))
    for h in range(H_C):
        o_ref[:, h * D_C:(h + 1) * D_C] = acc[h * t_new:(h + 1) * t_new, h * D_C:(h + 1) * D_C].astype(o_ref.dtype)


KEY_GROUP = max(dil for _, dil in DILATED_BRANCHES)
CK_C = 512


def _c_sample_plan(wlen, t, ck):
    dist = wlen + np.arange(t)[None, :] - np.arange(wlen)[:, None]
    need = np.zeros(wlen, bool)
    for window, dil in DILATED_BRANCHES:
        need |= ((dist % dil == 0) & (dist <= window)).any(axis=1)
    nck = wlen // ck
    first_half = (np.arange(wlen) % KEY_GROUP) < KEY_GROUP // 2
    n_half = 0
    while n_half < nck and not (need & ~first_half)[n_half * ck:(n_half + 1) * ck].any():
        n_half += 1
    keep = first_half | (np.arange(wlen) >= n_half * ck)
    return n_half, np.nonzero(keep)[0]


def _attn_c_sample(zb_s, cache_k, cache_v, o_idx, bias_past, bias_new, *, ck=CK_C):
    bsz, t, d3 = zb_s.shape
    d = d3 // 3
    wlen = cache_k.shape[2]
    nck = wlen // ck
    nbr = bias_past.shape[0]
    n_half, kidx = _c_sample_plan(wlen, t, ck)
    nrows = len(kidx)
    assert H_C * t == LANE and wlen % ck == 0 and ck % (2 * KEY_GROUP) == 0
    assert 1 <= n_half < nck and bias_past.shape[1] == nrows
    qt = jnp.swapaxes(zb_s[:, :, :d], 1, 2)
    rows = np.arange(d)[:, None] // D_C
    cols = np.arange(LANE)[None, :] // t
    wq = jnp.where(jnp.asarray(rows == cols), jnp.tile(qt, (1, 1, H_C)), jnp.zeros((), BF16))
    kn = zb_s[:, :, d:2 * d]
    vn = zb_s[:, :, 2 * d:]
    grp = KEY_GROUP * H_C
    full = lambda a: a.reshape(a.shape[0], bsz, wlen * H_C, D_C)
    half = lambda a: a.reshape(a.shape[0], bsz, wlen // KEY_GROUP, grp, D_C)
    n_full = nck - n_half
    assert n_half % n_full == 0
    half_spec = pl.BlockSpec((None, None, n_half * ck // KEY_GROUP, grp // 2, D_C), lambda b: (o_idx, b, 0, 0, 0))
    full_spec = pl.BlockSpec((None, None, n_full * ck * H_C, D_C), lambda b: (o_idx, b, n_half // n_full, 0))
    return pl.pallas_call(
        functools.partial(_attn_c_sample_kernel, n_half=n_half, n_full=n_full, ck=ck),
        grid=(bsz,),
        in_specs=[
            pl.BlockSpec((None, d, LANE), lambda b: (b, 0, 0)),
            pl.BlockSpec((None, t, d), lambda b: (b, 0, 0)),
            pl.BlockSpec((None, t, d), lambda b: (b, 0, 0)),
            pl.BlockSpec((nbr, nrows, LANE), lambda b: (0, 0, 0)),
            pl.BlockSpec((nbr, t, LANE), lambda b: (0, 0, 0)),
            half_spec, full_spec, half_spec, full_spec,
        ],
        out_specs=pl.BlockSpec((None, t, d), lambda b: (b, 0, 0)),
        out_shape=jax.ShapeDtypeStruct((bsz, t, d), BF16),
        scratch_shapes=[pltpu.VMEM((nrows, LANE), F32), pltpu.VMEM((nrows, LANE), BF16)],
        compiler_params=_params("parallel"),
        name="attn_c_sample",
    )(wq, kn, vn, bias_past, bias_new, half(cache_k), full(cache_k), half(cache_v), full(cache_v))


def _bias_c_sample(rel_bias, wlen, t):
    _, kidx = _c_sample_plan(wlen, t, CK_C)
    nrows = len(kidx)
    krow = np.concatenate([kidx, wlen + np.arange(t)])[:, None]
    col = np.arange(LANE)[None, :]
    dist = wlen + (col % t) - krow
    idx = []
    for window, dil in DILATED_BRANCHES:
        ok = (dist >= 0) & (dist % dil == 0) & (dist <= window)
        idx.append(np.where(ok, _bucket_np(dist), MASKED))
    idx = np.stack(idx).astype(np.int32)
    pad = (-idx.shape[1]) % 8
    idx = np.pad(idx, ((0, 0), (0, pad), (0, 0)), constant_values=MASKED)
    tab = jnp.broadcast_to(_head_table(rel_bias, np.arange(LANE) // t)[None], (idx.shape[0], N_BUCKETS + 1, LANE))
    out = _bias_expand(idx, tab, tr=idx.shape[1])
    return out[:, :nrows], out[:, nrows:nrows + t]


def kernel(x_prompt, x_sample, cache_k_a, cache_v_a, page_table, cache_k_c, cache_v_c, rel_bias, norm_gains, w_ffn_gate, w_ffn_up, w_ffn_down, w_in_ab, w_out_ab, lambda_qk, subln_gain, ln_v_gain, ln_v_bias, w_spatial, b_spatial, w_in_c, w_out_c):
    bp, s_len, d = x_prompt.shape
    bs, t_new, _ = x_sample.shape
    assert bp == 1
    n_s = bs * t_new
    depth = norm_gains.shape[0]
    xp = x_prompt.reshape(s_len, d)
    xs = x_sample.reshape(n_s, d)

    tq_a = 512
    past_a = page_table.shape[1] * cache_k_a.shape[2]
    bias_a_prompt = _bias_tiles_a_prompt(rel_bias, tq_a)
    bias_a_past, bias_a_new = _bias_a_sample(rel_bias, past_a, t_new)
    bias_c_prompt = _bias_tiles_c_prompt(rel_bias)
    bias_c_past, bias_c_new = _bias_c_sample(rel_bias, cache_k_c.shape[2], t_new)

    wg_all, wu_all, wd_all = w_ffn_gate.astype(BF16), w_ffn_up.astype(BF16), w_ffn_down.astype(BF16)
    ka_p, va_p, ka_s, va_s, vb_p, vb_s = [], [], [], [], [], []
    kc_p, vc_p, kc_s, vc_s = [], [], [], []
    for li in range(depth):
        ng = norm_gains[li]

        def half_ffn(xp, xs, j):
            wts = (ng[2 * j, 0], ng[2 * j, 1], wg_all, wu_all, wd_all, li, j)
            return _ffn_half(xp, *wts), _ffn_half(xs, *wts)

        xp, xs = half_ffn(xp, xs, 0)
        if li % 2 == 0:
            e = li // 2
            w = H_A * 2 * D_QK
            proj = (ng[1, 0], w_in_ab[e].astype(BF16), ln_v_gain[e], ln_v_bias[e])
            k_p, v_p, u_p, g_p, zb_p = _proj_ab(xp, *proj)
            k_s, v_s, u_s, g_s, zb_s = _proj_ab(xs, *proj)
            ka_p.append(k_p.reshape(1, s_len, H_A, 2 * D_QK))
            va_p.append(v_p.reshape(1, s_len, H_A, 2 * D_QK))
            ka_s.append(k_s.reshape(bs, t_new, H_A, 2 * D_QK))
            va_s.append(v_s.reshape(bs, t_new, H_A, 2 * D_QK))
            last = s_len - ((s_len - 1) // CHUNK_B) * CHUNK_B
            vb_p.append(g_p[s_len - last:].reshape(1, last, w))
            vb_s.append(g_s.reshape(bs, t_new, w))

            o_p = _attn_a_prompt(zb_p, bias_a_prompt, lambda_qk[e], subln_gain[e], li, tq=tq_a)
            o_s = _attn_a_sample(zb_s.reshape(bs, t_new, 3 * w), cache_k_a, cache_v_a, page_table, e,
                                 bias_a_past, bias_a_new, lambda_qk[e], subln_gain[e], li)
            gate_p = _gate_prompt(u_p, g_p, w_spatial[e], b_spatial[e])
            gate_s = _gate_sample(u_s.reshape(bs, t_new, w), g_s.reshape(bs, t_new, w), w_spatial[e], b_spatial[e])
            w_out = w_out_ab[e].astype(BF16)
            xp = _out_proj(o_p, gate_p, w_out, xp, ng[1, 1])
            xs = _out_proj(o_s.reshape(n_s, w), gate_s.reshape(n_s, w), w_out, xs, ng[1, 1])
        else:
            o = li // 2
            w_in = w_in_c[o].astype(BF16)
            zf_p = _proj_c(xp, ng[1, 0], w_in)
            zf_s = _proj_c(xs, ng[1, 0], w_in)
            keep = min(max(wd for wd, _ in DILATED_BRANCHES), s_len)
            kc_p.append(zf_p[s_len - keep:, d:2 * d].reshape(1, keep, H_C, D_C))
            vc_p.append(zf_p[s_len - keep:, 2 * d:].reshape(1, keep, H_C, D_C))
            kc_s.append(zf_s[:, d:2 * d].reshape(bs, t_new, H_C, D_C))
            vc_s.append(zf_s[:, 2 * d:].reshape(bs, t_new, H_C, D_C))
            o_p = _attn_c_prompt(zf_p, s_len, bias_c_prompt)
            o_s = _attn_c_sample(zf_s.astype(BF16).reshape(bs, t_new, 3 * d), cache_k_c, cache_v_c, o,
                                 bias_c_past, bias_c_new).reshape(n_s, d)
            w_out = w_out_c[o].astype(BF16)
            xp = _out_proj(o_p, o_p, w_out, xp, ng[1, 1], a_blk=0, b_blk=1)
            xs = _out_proj(o_s, o_s, w_out, xs, ng[1, 1], a_blk=0, b_blk=1)
        xp, xs = half_ffn(xp, xs, 1)

    y_prompt = xp.reshape(1, s_len, d)
    y_sample = xs.reshape(bs, t_new, d)
    return (y_prompt, y_sample,
            jnp.stack(ka_p, axis=1), jnp.stack(va_p, axis=1), jnp.stack(ka_s, axis=1), jnp.stack(va_s, axis=1),
            jnp.stack(vb_p, axis=0), jnp.stack(vb_s, axis=0),
            jnp.stack(kc_p, axis=0), jnp.stack(vc_p, axis=0), jnp.stack(kc_s, axis=0), jnp.stack(vc_s, axis=0))
```

```python
import functools
import math

import numpy as np
import jax
import jax.numpy as jnp
from jax import lax
from jax.experimental import pallas as pl
from jax.experimental.pallas import tpu as pltpu

F32 = jnp.float32
BF16 = jnp.bfloat16

EPS = 1e-6
NEG_INF = -1e30
LOG2E = math.log2(math.e)
LANE = 128
VMEM_LIMIT = 56 * 1024 * 1024

H_A = 8
D_QK = 64
G_B = 8
C_B = 128
CHUNK_B = 128
H_C = 16
D_C = 128
DILATED_BRANCHES = ((128, 1), (512, 4), (2048, 16))
N_BUCKETS = 32
MAX_DISTANCE = 128
MASKED = N_BUCKETS


def _params(*sem):
    return pltpu.CompilerParams(dimension_semantics=sem, vmem_limit_bytes=VMEM_LIMIT)


def _rms(x, g):
    return x * lax.rsqrt(jnp.mean(x * x, axis=-1, keepdims=True) + EPS) * g


def _gelu(x):
    return 0.5 * x * (1.0 + lax.erf(x * np.float32(math.sqrt(0.5))))


def _dot(a, b):
    return jnp.dot(a, b, preferred_element_type=F32)


def _dot_nt(a, b):
    return lax.dot_general(a, b, (((1,), (1,)), ((), ())), preferred_element_type=F32)


def _dot_tn(a, b):
    return lax.dot_general(a, b, (((0,), (0,)), ((), ())), preferred_element_type=F32)


def _lambda_init(layer):
    return 0.8 - 0.6 * math.exp(-0.3 * layer)


def _diff_lambda(lam_ref, layer):
    lp = lam_ref[...]
    a = jnp.sum(lp[0:1] * lp[1:2], axis=-1, keepdims=True)
    b = jnp.sum(lp[2:3] * lp[3:4], axis=-1, keepdims=True)
    return jnp.exp(a) - jnp.exp(b) + np.float32(_lambda_init(layer))


def _ffn_body(f, x_ref, gpre_ref, gpost_ref, wg, wu, wd, o_ref, h_ref):
    @pl.when(f == 0)
    def _():
        h_ref[...] = _rms(x_ref[...], gpre_ref[...]).astype(BF16)
        o_ref[...] = jnp.zeros_like(o_ref)

    h = h_ref[...]
    a = _dot(h, wg)
    b = _dot(h, wu)
    s = (a * jax.nn.sigmoid(a)) * b
    o_ref[...] += _dot(s.astype(BF16), wd)

    @pl.when(f == pl.num_programs(1) - 1)
    def _():
        o_ref[...] = x_ref[...] + 0.5 * _rms(o_ref[...], gpost_ref[...])


def _ffn_kernel(x_ref, gpre_ref, gpost_ref, wg_ref, wu_ref, wd_ref, o_ref, h_ref):
    _ffn_body(pl.program_id(1), x_ref, gpre_ref, gpost_ref, wg_ref[...], wu_ref[...], wd_ref[...], o_ref, h_ref)


def _ffn_half(x, g_pre, g_post, wg, wu, wd, *, tm=512, tf=512):
    n, d = x.shape
    f = wg.shape[1]
    return pl.pallas_call(
        _ffn_kernel,
        grid=(n // tm, f // tf),
        in_specs=[
            pl.BlockSpec((tm, d), lambda i, j: (i, 0)),
            pl.BlockSpec((1, d), lambda i, j: (0, 0)),
            pl.BlockSpec((1, d), lambda i, j: (0, 0)),
            pl.BlockSpec((d, tf), lambda i, j: (0, j)),
            pl.BlockSpec((d, tf), lambda i, j: (0, j)),
            pl.BlockSpec((tf, d), lambda i, j: (j, 0)),
        ],
        out_specs=pl.BlockSpec((tm, d), lambda i, j: (i, 0)),
        out_shape=jax.ShapeDtypeStruct((n, d), F32),
        scratch_shapes=[pltpu.VMEM((tm, d), BF16)],
        compiler_params=_params("parallel", "arbitrary"),
        name="ffn_half",
    )(x, g_pre.reshape(1, d), g_post.reshape(1, d), wg, wu, wd)


def _ffn_cast_kernel(x_ref, gpre_ref, gpost_ref, wg_ref, wu_ref, wd_ref, o_ref, wgb_ref, wub_ref, wdb_ref, h_ref):
    wg = wg_ref[...].astype(BF16)
    wu = wu_ref[...].astype(BF16)
    wd = wd_ref[...].astype(BF16)
    wgb_ref[...] = wg
    wub_ref[...] = wu
    wdb_ref[...] = wd
    _ffn_body(pl.program_id(1), x_ref, gpre_ref, gpost_ref, wg, wu, wd, o_ref, h_ref)


def _ffn_half_cast(x, g_pre, g_post, wg, wu, wd, li, hj, *, tf=256):
    n, d = x.shape
    f = wg.shape[-1]
    once = pl.Buffered(1)
    out, wgb, wub, wdb = pl.pallas_call(
        _ffn_cast_kernel,
        grid=(1, f // tf),
        in_specs=[
            pl.BlockSpec((n, d), lambda i, j: (0, 0), pipeline_mode=once),
            pl.BlockSpec((1, d), lambda i, j: (0, 0)),
            pl.BlockSpec((1, d), lambda i, j: (0, 0)),
            pl.BlockSpec((None, None, d, tf), lambda i, j: (li, hj, 0, j)),
            pl.BlockSpec((None, None, d, tf), lambda i, j: (li, hj, 0, j)),
            pl.BlockSpec((None, None, tf, d), lambda i, j: (li, hj, j, 0)),
        ],
        out_specs=[
            pl.BlockSpec((n, d), lambda i, j: (0, 0)),
            pl.BlockSpec((d, tf), lambda i, j: (0, j)),
            pl.BlockSpec((d, tf), lambda i, j: (0, j)),
            pl.BlockSpec((tf, d), lambda i, j: (j, 0)),
        ],
        out_shape=[jax.ShapeDtypeStruct((n, d), F32), jax.ShapeDtypeStruct((d, f), BF16),
                   jax.ShapeDtypeStruct((d, f), BF16), jax.ShapeDtypeStruct((f, d), BF16)],
        scratch_shapes=[pltpu.VMEM((n, d), BF16)],
        compiler_params=_params("arbitrary", "arbitrary"),
        name="ffn_half_cast",
    )(x, g_pre.reshape(1, d), g_post.reshape(1, d), wg, wu, wd)
    return out, (wgb, wub, wdb)


def _proj_ab_kernel(x_ref, g_ref, w_ref, lng_ref, lnb_ref, k_ref, v_ref, u_ref, gv_ref, zb_ref, h_ref):
    j = pl.program_id(1)

    @pl.when(j == 0)
    def _():
        h_ref[...] = _rms(x_ref[...], g_ref[...]).astype(BF16)

    z = _dot(h_ref[...], w_ref[...])

    @pl.when(j == 0)
    def _():
        zb_ref[...] = (z * np.float32(D_QK ** -0.5 * LOG2E)).astype(BF16)

    @pl.when(j == 1)
    def _():
        k_ref[...] = z
        zb_ref[...] = z.astype(BF16)

    @pl.when(j == 2)
    def _():
        v_ref[...] = z
        zb_ref[...] = z.astype(BF16)

    @pl.when(j == 3)
    def _():
        u_ref[...] = _gelu(z)

    @pl.when(j == 4)
    def _():
        a = _gelu(z)
        c = a - jnp.mean(a, axis=-1, keepdims=True)
        y = c * lax.rsqrt(jnp.mean(c * c, axis=-1, keepdims=True) + EPS)
        gv_ref[...] = y * lng_ref[...] + lnb_ref[...]


def _proj_ab(x, g_pre, w_in, ln_g, ln_b, *, tm=512):
    n, d = x.shape
    w = w_in.shape[1] // 5
    row = pl.BlockSpec((tm, w), lambda i, j: (i, 0))
    return pl.pallas_call(
        _proj_ab_kernel,
        grid=(n // tm, 5),
        in_specs=[
            pl.BlockSpec((tm, d), lambda i, j: (i, 0)),
            pl.BlockSpec((1, d), lambda i, j: (0, 0)),
            pl.BlockSpec((d, w), lambda i, j: (0, j)),
            pl.BlockSpec((1, w), lambda i, j: (0, 0)),
            pl.BlockSpec((1, w), lambda i, j: (0, 0)),
        ],
        out_specs=[row, row, row, row, pl.BlockSpec((tm, w), lambda i, j: (i, jnp.minimum(j, 2)))],
        out_shape=[jax.ShapeDtypeStruct((n, w), F32)] * 4 + [jax.ShapeDtypeStruct((n, 3 * w), BF16)],
        scratch_shapes=[pltpu.VMEM((tm, d), BF16)],
        compiler_params=_params("parallel", "arbitrary"),
        name="proj_ab",
    )(x, g_pre.reshape(1, d), w_in, ln_g.reshape(1, w), ln_b.reshape(1, w))


def _proj_c_kernel(x_ref, g_ref, w_ref, zf_ref, h_ref):
    j = pl.program_id(1)

    @pl.when(j == 0)
    def _():
        h_ref[...] = _rms(x_ref[...], g_ref[...]).astype(BF16)

    z = _dot(h_ref[...], w_ref[...])

    @pl.when(j == 0)
    def _():
        zf_ref[...] = z * np.float32(D_C ** -0.5)

    @pl.when(j > 0)
    def _():
        zf_ref[...] = z


def _proj_c(x, g_pre, w_in, *, tm=512):
    n, d = x.shape
    w = w_in.shape[1] // 3
    return pl.pallas_call(
        _proj_c_kernel,
        grid=(n // tm, 3),
        in_specs=[
            pl.BlockSpec((tm, d), lambda i, j: (i, 0)),
            pl.BlockSpec((1, d), lambda i, j: (0, 0)),
            pl.BlockSpec((d, w), lambda i, j: (0, j)),
        ],
        out_specs=pl.BlockSpec((tm, w), lambda i, j: (i, j)),
        out_shape=jax.ShapeDtypeStruct((n, 3 * w), F32),
        scratch_shapes=[pltpu.VMEM((tm, d), BF16)],
        compiler_params=_params("parallel", "arbitrary"),
        name="proj_c",
    )(x, g_pre.reshape(1, d), w_in)


def _out_proj_kernel(a_ref, b_ref, wa_ref, wb_ref, x_ref, g_ref, o_ref):
    y = _dot(a_ref[...].astype(BF16), wa_ref[...]) + _dot(b_ref[...].astype(BF16), wb_ref[...])
    o_ref[...] = x_ref[...] + _rms(y, g_ref[...])


def _out_proj(a, b, w_out, x, g_post, *, a_blk=0, b_blk=0, tm=512):
    n, d = x.shape
    kh = w_out.shape[0] // 2
    return pl.pallas_call(
        _out_proj_kernel,
        grid=(n // tm,),
        in_specs=[
            pl.BlockSpec((tm, kh), lambda i: (i, a_blk)),
            pl.BlockSpec((tm, kh), lambda i: (i, b_blk)),
            pl.BlockSpec((kh, d), lambda i: (0, 0)),
            pl.BlockSpec((kh, d), lambda i: (1, 0)),
            pl.BlockSpec((tm, d), lambda i: (i, 0)),
            pl.BlockSpec((1, d), lambda i: (0, 0)),
        ],
        out_specs=pl.BlockSpec((tm, d), lambda i: (i, 0)),
        out_shape=jax.ShapeDtypeStruct((n, d), F32),
        compiler_params=_params("parallel"),
        name="out_proj",
    )(a, b, w_out, w_out, x, g_post.reshape(1, d))


def _bucket_np(dist):
    max_exact = N_BUCKETS // 2
    d = np.maximum(dist, 0)
    ratio = np.log(np.maximum(d, 1).astype(np.float32) / np.float32(max_exact)) / np.float32(math.log(MAX_DISTANCE / max_exact))
    large = np.minimum(max_exact + (ratio * (N_BUCKETS - max_exact)).astype(np.int32), N_BUCKETS - 1)
    return np.where(d < max_exact, d, large).astype(np.int32)


def _bias_expand_kernel(idx_ref, tab_ref, o_ref):
    idx = idx_ref[...]
    val = jnp.zeros(idx.shape, F32)
    for b in range(N_BUCKETS + 1):
        val = jnp.where(idx == b, tab_ref[b:b + 1, :], val)
    o_ref[...] = val


def _bias_expand(idx, tab, *, tr=256):
    gi, r, c = idx.shape
    g = tab.shape[0]
    tr = min(tr, r)
    return pl.pallas_call(
        _bias_expand_kernel,
        grid=(g, r // tr),
        in_specs=[
            pl.BlockSpec((None, tr, c), (lambda a, i: (a, i, 0)) if gi == g else (lambda a, i: (0, i, 0))),
            pl.BlockSpec((None, N_BUCKETS + 1, c), lambda a, i: (a, 0, 0)),
        ],
        out_specs=pl.BlockSpec((None, tr, c), lambda a, i: (a, i, 0)),
        out_shape=jax.ShapeDtypeStruct((g, r, c), F32),
        compiler_params=_params("parallel", "parallel"),
        name="bias_expand",
    )(jnp.asarray(idx), tab)


def _head_table(rel_bias, heads_of_col):
    t = rel_bias.astype(F32)[:, np.asarray(heads_of_col)]
    return jnp.concatenate([t, jnp.full((1, t.shape[1]), NEG_INF, F32)], axis=0)


def _attn_a_prompt_kernel(q_ref, k_ref, v_ref, bias_ref, lam_ref, sg_ref, o_ref, m_ref, l_ref, acc_ref, *, tq, layer):
    qi = pl.program_id(1)
    q = q_ref[...]
    lane = lax.broadcasted_iota(jnp.int32, q.shape, 1)
    zero = jnp.zeros_like(q)
    qq = jnp.concatenate([jnp.where(lane < D_QK, q, zero), jnp.where(lane >= D_QK, q, zero)], axis=0)
    m_ref[...] = jnp.full(m_ref.shape, NEG_INF, F32)
    l_ref[...] = jnp.zeros(l_ref.shape, F32)
    acc_ref[...] = jnp.zeros(acc_ref.shape, F32)

    def tile(j, bias_tile):
        off = pl.multiple_of(j * tq, tq)
        s = _dot_nt(qq, k_ref[pl.ds(off, tq), :])
        if bias_tile is not None:
            b = bias_ref[bias_tile]
            s = s + jnp.concatenate([b, b], axis=0)
        st = [s[:, c * LANE:(c + 1) * LANE] for c in range(tq // LANE)]
        m_prev = m_ref[...]
        m_new = jnp.maximum(m_prev, jnp.max(functools.reduce(jnp.maximum, st), axis=-1, keepdims=True))
        alpha = jnp.exp2(m_prev - m_new)
        ps = [jnp.exp2(t - m_new) for t in st]
        l_ref[...] = alpha * l_ref[...] + functools.reduce(lambda a, b: a + b, ps)
        p = jnp.concatenate([x.astype(BF16) for x in ps], axis=1)
        acc_ref[...] = alpha * acc_ref[...] + _dot(p, v_ref[pl.ds(off, tq), :])
        m_ref[...] = m_new

    def far_body(j, carry):
        tile(j, None)
        return carry

    lax.fori_loop(0, qi - 1, far_body, 0)

    @pl.when(qi >= 1)
    def _():
        tile(qi - 1, 1)

    tile(qi, 0)
    lam = _diff_lambda(lam_ref, layer)
    o = acc_ref[...] / jnp.sum(l_ref[...], axis=-1, keepdims=True)
    o = o[:tq] - lam * o[tq:]
    o_ref[...] = (_rms(o, sg_ref[...]) * np.float32(1.0 - _lambda_init(layer))).astype(o_ref.dtype)


def _attn_a_prompt(zb, bias_tiles, lam_p, subln_g, layer, *, tq=512):
    s_len = zb.shape[0]
    w = zb.shape[1] // 3
    hw = w // H_A
    return pl.pallas_call(
        functools.partial(_attn_a_prompt_kernel, tq=tq, layer=layer),
        grid=(H_A, s_len // tq),
        in_specs=[
            pl.BlockSpec((tq, hw), lambda h, i: (i, h)),
            pl.BlockSpec((s_len, hw), lambda h, i: (0, H_A + h)),
            pl.BlockSpec((s_len, hw), lambda h, i: (0, 2 * H_A + h)),
            pl.BlockSpec((None, 2, tq, tq), lambda h, i: (h, 0, 0, 0)),
            pl.BlockSpec((4, D_QK), lambda h, i: (0, 0)),
            pl.BlockSpec((1, hw), lambda h, i: (0, 0)),
        ],
        out_specs=pl.BlockSpec((tq, hw), lambda h, i: (i, h)),
        out_shape=jax.ShapeDtypeStruct((s_len, w), BF16),
        scratch_shapes=[pltpu.VMEM((2 * tq, LANE), F32), pltpu.VMEM((2 * tq, LANE), F32), pltpu.VMEM((2 * tq, hw), F32)],
        compiler_params=_params("parallel", "arbitrary"),
        name="attn_a_prompt",
    )(zb, zb, zb, bias_tiles, lam_p, subln_g.reshape(1, hw))


def _bias_tiles_a_prompt(rel_bias, tq):
    assert _bucket_np(np.array([tq + 1]))[0] == N_BUCKETS - 1
    i = np.arange(tq)[:, None]
    j = np.arange(tq)[None, :]
    idx = []
    for delta in (0, tq):
        dist = delta + i - j
        idx.append(np.where(dist >= 0, _bucket_np(dist), MASKED))
    idx = np.stack(idx).reshape(1, 2 * tq, tq).astype(np.int32)
    shifted = (rel_bias - rel_bias[N_BUCKETS - 1:]) * np.float32(LOG2E)
    tab = jnp.stack([_head_table(shifted, np.full(tq, h)) for h in range(H_A)])
    return _bias_expand(idx, tab, tr=min(tq, 256)).reshape(H_A, 2, tq, tq)


def _block_diag_queries(q, group_width):
    t = q.shape[0]
    tiled = jnp.concatenate([q] * (LANE // t), axis=0).astype(F32)
    row_g = lax.broadcasted_iota(jnp.int32, tiled.shape, 0) // t
    col_g = lax.broadcasted_iota(jnp.int32, tiled.shape, 1) // group_width
    return jnp.where(row_g == col_g, tiled, 0.0).T.astype(BF16)


def _attn_a_sample_kernel(pt_ref, q_ref, kn_ref, vn_ref, bp_ref, bn_ref, lam_ref, sg_ref, *rest, n_pages, page, layer):
    k_pages = rest[:n_pages]
    v_pages = rest[n_pages:2 * n_pages]
    o_ref = rest[2 * n_pages]
    s_ref, p_ref = rest[2 * n_pages + 1:]
    wq = _block_diag_queries(q_ref[...], D_QK)

    def heads_to_lanes(ref):
        return jnp.concatenate([ref[pl.ds(h, page, stride=H_A), :] for h in range(H_A)], axis=1).astype(BF16)

    for p in range(n_pages):
        s_ref[p * page:(p + 1) * page, :] = _dot(heads_to_lanes(k_pages[p]), wq) + bp_ref[p * page:(p + 1) * page, :]
    s_new = _dot(kn_ref[...], wq) + bn_ref[...]
    s_past = s_ref[...]
    m = jnp.maximum(jnp.max(s_past, axis=0, keepdims=True), jnp.max(s_new, axis=0, keepdims=True))
    e_past = jnp.exp2(s_past - m)
    e_new = jnp.exp2(s_new - m)
    den = jnp.sum(e_past, axis=0, keepdims=True) + jnp.sum(e_new, axis=0, keepdims=True)
    col = lax.broadcasted_iota(jnp.int32, den.shape, 1)
    t_new = kn_ref.shape[0]
    lam = _diff_lambda(lam_ref, layer)
    scale = jnp.where((col // t_new) % 2 == 0, 1.0, -lam) / den
    p_ref[...] = (e_past * scale).astype(BF16)
    acc = _dot_tn((e_new * scale).astype(BF16), vn_ref[...])
    for p in range(n_pages):
        acc = acc + _dot_tn(p_ref[p * page:(p + 1) * page, :], heads_to_lanes(v_pages[p]))
    for h in range(H_A):
        r0 = 2 * h * t_new
        o = acc[r0:r0 + t_new, h * LANE:(h + 1) * LANE] + acc[r0 + t_new:r0 + 2 * t_new, h * LANE:(h + 1) * LANE]
        o_ref[:, h * LANE:(h + 1) * LANE] = (_rms(o, sg_ref[...]) * np.float32(1.0 - _lambda_init(layer))).astype(o_ref.dtype)


def _attn_a_sample(zb_s, cache_k, cache_v, page_table, e, bias_past, bias_new, lam_p, subln_g, layer):
    bsz, t, w3 = zb_s.shape
    w = w3 // 3
    n_pages = page_table.shape[1]
    page = cache_k.shape[2]
    assert 2 * H_A * t == LANE

    cache_k = cache_k.reshape(cache_k.shape[0], cache_k.shape[1], page * H_A, LANE)
    cache_v = cache_v.reshape(cache_v.shape[0], cache_v.shape[1], page * H_A, LANE)

    def page_spec(p):
        return pl.BlockSpec((None, None, page * H_A, LANE), lambda b, pt: (pt[b * n_pages + p], e, 0, 0))

    grid_spec = pltpu.PrefetchScalarGridSpec(
        num_scalar_prefetch=1,
        grid=(bsz,),
        in_specs=[
            pl.BlockSpec((None, t, w), lambda b, pt: (b, 0, 0)),
            pl.BlockSpec((None, t, w), lambda b, pt: (b, 0, 1)),
            pl.BlockSpec((None, t, w), lambda b, pt: (b, 0, 2)),
            pl.BlockSpec((n_pages * page, LANE), lambda b, pt: (0, 0)),
            pl.BlockSpec((t, LANE), lambda b, pt: (0, 0)),
            pl.BlockSpec((4, D_QK), lambda b, pt: (0, 0)),
            pl.BlockSpec((1, LANE), lambda b, pt: (0, 0)),
        ] + [page_spec(p) for p in range(n_pages)] + [page_spec(p) for p in range(n_pages)],
        out_specs=pl.BlockSpec((None, t, w), lambda b, pt: (b, 0, 0)),
        scratch_shapes=[pltpu.VMEM((n_pages * page, LANE), F32), pltpu.VMEM((n_pages * page, LANE), BF16)],
    )
    return pl.pallas_call(
        functools.partial(_attn_a_sample_kernel, n_pages=n_pages, page=page, layer=layer),
        grid_spec=grid_spec,
        out_shape=jax.ShapeDtypeStruct((bsz, t, w), BF16),
        compiler_params=_params("arbitrary"),
        name="attn_a_sample",
    )(page_table.reshape(-1), zb_s, zb_s, zb_s, bias_past, bias_new, lam_p, subln_g.reshape(1, LANE),
      *([cache_k] * n_pages), *([cache_v] * n_pages))


def _bias_a_sample(rel_bias, past, t):
    kpos = np.arange(past + t)[:, None]
    col = np.arange(LANE)[None, :]
    dist = past + (col % t) - kpos
    idx = np.where(dist >= 0, _bucket_np(dist), MASKED).astype(np.int32)
    pad = (-idx.shape[0]) % 8
    idx = np.pad(idx, ((0, pad), (0, 0)), constant_values=MASKED)[None]
    tab = _head_table(rel_bias * np.float32(LOG2E), np.arange(LANE) // (2 * t))[None]
    out = _bias_expand(idx, tab, tr=idx.shape[1])[0]
    return out[:past], out[past:past + t]


def _gate_prompt_kernel(u_ref, g_ref, w_ref, b_ref, o_ref, *, chunks):
    row = lax.broadcasted_iota(jnp.int32, (CHUNK_B, CHUNK_B), 0)
    col = lax.broadcasted_iota(jnp.int32, (CHUNK_B, CHUNK_B), 1)
    for gi in range(G_B):
        wg = jnp.where(row >= col, w_ref[gi], 0.0).astype(BF16)
        bg = b_ref[:, gi:gi + 1]
        for c in range(chunks):
            rs = slice(c * CHUNK_B, (c + 1) * CHUNK_B)
            cs = slice(gi * C_B, (gi + 1) * C_B)
            mixed = _dot(wg, g_ref[rs, cs].astype(BF16)) + bg
            o_ref[rs, cs] = (u_ref[rs, cs] * mixed).astype(o_ref.dtype)


def _gate_prompt(u, g, w_s, b_s, *, chunks=4):
    s_len, w = u.shape
    tm = chunks * CHUNK_B
    return pl.pallas_call(
        functools.partial(_gate_prompt_kernel, chunks=chunks),
        grid=(s_len // tm,),
        in_specs=[
            pl.BlockSpec((tm, w), lambda i: (i, 0)),
            pl.BlockSpec((tm, w), lambda i: (i, 0)),
            pl.BlockSpec((G_B, CHUNK_B, CHUNK_B), lambda i: (0, 0, 0)),
            pl.BlockSpec((CHUNK_B, G_B), lambda i: (0, 0)),
        ],
        out_specs=pl.BlockSpec((tm, w), lambda i: (i, 0)),
        out_shape=jax.ShapeDtypeStruct((s_len, w), BF16),
        compiler_params=_params("parallel"),
        name="gate_prompt",
    )(u, g, w_s, b_s.T)


def _gate_sample_kernel(u_ref, g_ref, w_ref, b_ref, o_ref):
    t = u_ref.shape[1]
    g = g_ref[...]
    mixed = jnp.zeros(g.shape, F32) + b_ref[...][None]
    for s in range(t):
        mixed = mixed + w_ref[:, s, :][None] * g[:, s:s + 1, :]
    o_ref[...] = (u_ref[...] * mixed).astype(o_ref.dtype)


def _gate_sample(u, g, w_s, b_s, *, tb=32):
    bsz, t, w = u.shape
    causal = np.tril(np.ones((t, t), bool))
    wt = jnp.where(jnp.asarray(causal)[None], w_s[:, :t, :t], 0.0)
    wt = jnp.repeat(jnp.transpose(wt, (1, 2, 0)), C_B, axis=2)
    bt = jnp.repeat(b_s[:, :t].T, C_B, axis=1)
    return pl.pallas_call(
        _gate_sample_kernel,
        grid=(bsz // tb,),
        in_specs=[
            pl.BlockSpec((tb, t, w), lambda i: (i, 0, 0)),
            pl.BlockSpec((tb, t, w), lambda i: (i, 0, 0)),
            pl.BlockSpec((t, t, w), lambda i: (0, 0, 0)),
            pl.BlockSpec((t, w), lambda i: (0, 0)),
        ],
        out_specs=pl.BlockSpec((tb, t, w), lambda i: (i, 0, 0)),
        out_shape=jax.ShapeDtypeStruct((bsz, t, w), BF16),
        compiler_params=_params("parallel"),
        name="gate_sample",
    )(u, g, wt, bt)


def _attn_c_prompt_kernel(q_ref, kp_ref, kc_ref, vp_ref, vc_ref, bias_ref, o_ref, ob_ref, lb_ref, *, sb_len):
    sb = pl.program_id(0)

    def rows(ref, start, n, dil):
        return ref[pl.ds(start, n), :] if dil == 1 else ref[pl.ds(start, n, stride=dil), :]

    for bi, (window, dil) in enumerate(DILATED_BRANCHES):
        nstep = window // dil
        span = dil * nstep
        col = lax.broadcasted_iota(jnp.int32, (1, 2 * nstep), 1)
        no_prev = jnp.where(jnp.logical_and(sb == 0, col < nstep), np.float32(NEG_INF), np.float32(0.0))
        for blk in range(sb_len // span):
            base = blk * span
            for r in range(dil):
                q = rows(q_ref, base + r, nstep, dil).astype(BF16)
                if blk == 0:
                    k_prev = rows(kp_ref, sb_len - span + r, nstep, dil)
                    v_prev = rows(vp_ref, sb_len - span + r, nstep, dil)
                else:
                    k_prev = rows(kc_ref, base - span + r, nstep, dil)
                    v_prev = rows(vc_ref, base - span + r, nstep, dil)
                k = jnp.concatenate([k_prev, rows(kc_ref, base + r, nstep, dil)], axis=0).astype(BF16)
                v = jnp.concatenate([v_prev, rows(vc_ref, base + r, nstep, dil)], axis=0).astype(BF16)
                s = _dot_nt(q, k) + bias_ref[bi]
                if blk == 0:
                    s = s + no_prev
                m = jnp.max(s, axis=-1, keepdims=True)
                e = jnp.exp(s - m)
                den = jnp.sum(e, axis=-1, keepdims=True)
                o = _dot(e.astype(BF16), v) / den
                lse = jnp.broadcast_to(m + jnp.log(den), o.shape)
                if dil == 1:
                    ob_ref[bi, pl.ds(base, nstep), :] = o
                    lb_ref[bi, pl.ds(base, nstep), :] = lse
                else:
                    ob_ref[bi, pl.ds(base + r, nstep, stride=dil), :] = o
                    lb_ref[bi, pl.ds(base + r, nstep, stride=dil), :] = lse
    ls = [lb_ref[b] for b in range(len(DILATED_BRANCHES))]
    mm = functools.reduce(jnp.maximum, ls)
    es = [jnp.exp(l - mm) for l in ls]
    den = functools.reduce(lambda a, b: a + b, es)
    out = functools.reduce(lambda a, b: a + b, [(e / den) * ob_ref[b] for b, e in enumerate(es)])
    o_ref[...] = out.astype(o_ref.dtype)


def _attn_c_prompt(zf, s_len, bias):
    d = zf.shape[1] // 3
    nbr = len(DILATED_BRANCHES)
    sb_len = max(window for window, _ in DILATED_BRANCHES)
    assert s_len % sb_len == 0 and all(sb_len % window == 0 for window, _ in DILATED_BRANCHES)
    nstep = bias.shape[2]
    prev = lambda part: (lambda i, h: (jnp.maximum(i - 1, 0), part * H_C + h))
    cur = lambda part: (lambda i, h: (i, part * H_C + h))
    return pl.pallas_call(
        functools.partial(_attn_c_prompt_kernel, sb_len=sb_len),
        grid=(s_len // sb_len, H_C),
        in_specs=[
            pl.BlockSpec((sb_len, D_C), cur(0)),
            pl.BlockSpec((sb_len, D_C), prev(1)),
            pl.BlockSpec((sb_len, D_C), cur(1)),
            pl.BlockSpec((sb_len, D_C), prev(2)),
            pl.BlockSpec((sb_len, D_C), cur(2)),
            pl.BlockSpec((nbr, None, nstep, 2 * nstep), lambda i, h: (0, h, 0, 0)),
        ],
        out_specs=pl.BlockSpec((sb_len, D_C), lambda i, h: (i, h)),
        out_shape=jax.ShapeDtypeStruct((s_len, d), BF16),
        scratch_shapes=[pltpu.VMEM((nbr, sb_len, D_C), F32), pltpu.VMEM((nbr, sb_len, D_C), F32)],
        compiler_params=_params("parallel", "arbitrary"),
        name="attn_c_prompt",
    )(zf, zf, zf, zf, zf, bias)


def _bias_tiles_c_prompt(rel_bias):
    idx = []
    for window, dil in DILATED_BRANCHES:
        nstep = window // dil
        step = nstep + np.arange(nstep)[:, None] - np.arange(2 * nstep)[None, :]
        band = (step >= 0) & (step <= nstep)
        idx.append(np.where(band, _bucket_np(np.clip(step, 0, nstep) * dil), MASKED))
    nstep = idx[0].shape[0]
    assert all(i.shape == (nstep, 2 * nstep) for i in idx)
    nbr = len(idx)
    idx = np.stack(idx).reshape(1, nbr * nstep, 2 * nstep).astype(np.int32)
    tab = jnp.stack([_head_table(rel_bias, np.full(2 * nstep, h)) for h in range(H_C)])
    out = _bias_expand(idx, tab, tr=nstep)
    return jnp.swapaxes(out.reshape(H_C, nbr, nstep, 2 * nstep), 0, 1)


def _attn_c_sample_kernel(q_ref, kn_ref, vn_ref, bp_ref, bn_ref, kh_ref, kf_ref, vh_ref, vf_ref, o_ref,
                          s_ref, p_ref, *, n_half, n_full, ck):
    t_new = kn_ref.shape[0]
    hk = ck // 2
    kept = KEY_GROUP // 2
    gpc = ck // KEY_GROUP

    def full_to_lanes(ref, c):
        return jnp.concatenate([ref[pl.ds(c * ck * H_C + h, ck, stride=H_C), :] for h in range(H_C)],
                               axis=1).astype(BF16)

    def half_to_lanes(ref, c):
        return jnp.concatenate(
            [jnp.concatenate([ref[g, pl.ds(h, kept, stride=H_C), :] for g in range(c * gpc, (c + 1) * gpc)], axis=0)
             for h in range(H_C)], axis=1).astype(BF16)

    chunks = [(half_to_lanes, c, slice(c * hk, (c + 1) * hk)) for c in range(n_half)]
    chunks += [(full_to_lanes, c, slice(n_half * hk + c * ck, n_half * hk + (c + 1) * ck)) for c in range(n_full)]

    wq = _block_diag_queries(q_ref[...], D_C)
    for to_lanes, c, rs in chunks:
        ref = kh_ref if to_lanes is half_to_lanes else kf_ref
        s_ref[rs, :] = _dot(to_lanes(ref, c), wq)
    s_past = s_ref[...]
    s_new = _dot(kn_ref[...], wq)
    es, lses = [], []
    for b in range(bp_ref.shape[0]):
        sp = s_past + bp_ref[b]
        sn = s_new + bn_ref[b]
        m = jnp.maximum(jnp.max(sp, axis=0, keepdims=True), jnp.max(sn, axis=0, keepdims=True))
        ep = jnp.exp(sp - m)
        en = jnp.exp(sn - m)
        den = jnp.sum(ep, axis=0, keepdims=True) + jnp.sum(en, axis=0, keepdims=True)
        es.append((ep, en, den))
        lses.append(m + jnp.log(den))
    mm = functools.reduce(jnp.maximum, lses)
    ws = [jnp.exp(l - mm) for l in lses]
    wsum = functools.reduce(lambda a, b: a + b, ws)
    pp = jnp.zeros(s_past.shape, F32)
    pn = jnp.zeros(s_new.shape, F32)
    for (ep, en, den), w in zip(es, ws):
        coef = (w / wsum) / den
        pp = pp + coef * ep
        pn = pn + coef * en
    p_ref[...] = pp.astype(BF16)
    acc = _dot_tn(pn.astype(BF16), vn_ref[...])
    for to_lanes, c, rs in chunks:
        ref = vh_ref if to_lanes is half_to_lanes else vf_ref
        acc = acc + _dot_tn(p_ref[rs, :], to_lanes(ref, c))
    for h in range(H_C):
        o_ref[:, h * D_C:(h + 1) * D_C] = acc[h * t_new:(h + 1) * t_new, h * D_C:(h + 1) * D_C].astype(o_ref.dtype)


KEY_GROUP = max(dil for _, dil in DILATED_BRANCHES)
CK_C = 512


def _c_sample_plan(wlen, t, ck):
    dist = wlen + np.arange(t)[None, :] - np.arange(wlen)[:, None]
    need = np.zeros(wlen, bool)
    for window, dil in DILATED_BRANCHES:
        need |= ((dist % dil == 0) & (dist <= window)).any(axis=1)
    nck = wlen // ck
    first_half = (np.arange(wlen) % KEY_GROUP) < KEY_GROUP // 2
    n_half = 0
    while n_half < nck and not (need & ~first_half)[n_half * ck:(n_half + 1) * ck].any():
        n_half += 1
    keep = first_half | (np.arange(wlen) >= n_half * ck)
    return n_half, np.nonzero(keep)[0]


def _attn_c_sample(zb_s, cache_k, cache_v, o_idx, bias_past, bias_new, *, ck=CK_C):
    bsz, t, d3 = zb_s.shape
    d = d3 // 3
    wlen = cache_k.shape[2]
    nck = wlen // ck
    nbr = bias_past.shape[0]
    n_half, kidx = _c_sample_plan(wlen, t, ck)
    nrows = len(kidx)
    assert H_C * t == LANE and wlen % ck == 0 and ck % (2 * KEY_GROUP) == 0
    assert 1 <= n_half < nck and bias_past.shape[1] == nrows
    grp = KEY_GROUP * H_C
    full = lambda a: a.reshape(a.shape[0], bsz, wlen * H_C, D_C)
    half = lambda a: a.reshape(a.shape[0], bsz, wlen // KEY_GROUP, grp, D_C)
    n_full = nck - n_half
    assert n_half % n_full == 0
    half_spec = pl.BlockSpec((None, None, n_half * ck // KEY_GROUP, grp // 2, D_C), lambda b: (o_idx, b, 0, 0, 0))
    full_spec = pl.BlockSpec((None, None, n_full * ck * H_C, D_C), lambda b: (o_idx, b, n_half // n_full, 0))
    return pl.pallas_call(
        functools.partial(_attn_c_sample_kernel, n_half=n_half, n_full=n_full, ck=ck),
        grid=(bsz,),
        in_specs=[
            pl.BlockSpec((None, t, d), lambda b: (b, 0, 0)),
            pl.BlockSpec((None, t, d), lambda b: (b, 0, 1)),
            pl.BlockSpec((None, t, d), lambda b: (b, 0, 2)),
            pl.BlockSpec((nbr, nrows, LANE), lambda b: (0, 0, 0)),
            pl.BlockSpec((nbr, t, LANE), lambda b: (0, 0, 0)),
            half_spec, full_spec, half_spec, full_spec,
        ],
        out_specs=pl.BlockSpec((None, t, d), lambda b: (b, 0, 0)),
        out_shape=jax.ShapeDtypeStruct((bsz, t, d), BF16),
        scratch_shapes=[pltpu.VMEM((nrows, LANE), F32), pltpu.VMEM((nrows, LANE), BF16)],
        compiler_params=_params("parallel"),
        name="attn_c_sample",
    )(zb_s, zb_s, zb_s, bias_past, bias_new, half(cache_k), full(cache_k), half(cache_v), full(cache_v))


def _bias_c_sample(rel_bias, wlen, t):
    _, kidx = _c_sample_plan(wlen, t, CK_C)
    nrows = len(kidx)
    krow = np.concatenate([kidx, wlen + np.arange(t)])[:, None]
    col = np.arange(LANE)[None, :]
    dist = wlen + (col % t) - krow
    idx = []
    for window, dil in DILATED_BRANCHES:
        ok = (dist >= 0) & (dist % dil == 0) & (dist <= window)
        idx.append(np.where(ok, _bucket_np(dist), MASKED))
    idx = np.stack(idx).astype(np.int32)
    pad = (-idx.shape[1]) % 8
    idx = np.pad(idx, ((0, 0), (0, pad), (0, 0)), constant_values=MASKED)
    tab = jnp.broadcast_to(_head_table(rel_bias, np.arange(LANE) // t)[None], (idx.shape[0], N_BUCKETS + 1, LANE))
    out = _bias_expand(idx, tab, tr=idx.shape[1])
    return out[:, :nrows], out[:, nrows:nrows + t]


def kernel(x_prompt, x_sample, cache_k_a, cache_v_a, page_table, cache_k_c, cache_v_c, rel_bias, norm_gains, w_ffn_gate, w_ffn_up, w_ffn_down, w_in_ab, w_out_ab, lambda_qk, subln_gain, ln_v_gain, ln_v_bias, w_spatial, b_spatial, w_in_c, w_out_c):
    bp, s_len, d = x_prompt.shape
    bs, t_new, _ = x_sample.shape
    assert bp == 1
    n_s = bs * t_new
    depth = norm_gains.shape[0]
    xp = x_prompt.reshape(s_len, d)
    xs = x_sample.reshape(n_s, d)

    tq_a = 512
    past_a = page_table.shape[1] * cache_k_a.shape[2]
    bias_a_prompt = _bias_tiles_a_prompt(rel_bias, tq_a)
    bias_a_past, bias_a_new = _bias_a_sample(rel_bias, past_a, t_new)
    bias_c_prompt = _bias_tiles_c_prompt(rel_bias)
    bias_c_past, bias_c_new = _bias_c_sample(rel_bias, cache_k_c.shape[2], t_new)

    ka_p, va_p, ka_s, va_s, vb_p, vb_s = [], [], [], [], [], []
    kc_p, vc_p, kc_s, vc_s = [], [], [], []
    for li in range(depth):
        ng = norm_gains[li]

        def half_ffn(xp, xs, j):
            xs, wts = _ffn_half_cast(xs, ng[2 * j, 0], ng[2 * j, 1], w_ffn_gate, w_ffn_up, w_ffn_down, li, j)
            return _ffn_half(xp, ng[2 * j, 0], ng[2 * j, 1], *wts), xs

        xp, xs = half_ffn(xp, xs, 0)
        if li % 2 == 0:
            e = li // 2
            w = H_A * 2 * D_QK
            proj = (ng[1, 0], w_in_ab[e].astype(BF16), ln_v_gain[e], ln_v_bias[e])
            k_p, v_p, u_p, g_p, zb_p = _proj_ab(xp, *proj)
            k_s, v_s, u_s, g_s, zb_s = _proj_ab(xs, *proj)
            ka_p.append(k_p.reshape(1, s_len, H_A, 2 * D_QK))
            va_p.append(v_p.reshape(1, s_len, H_A, 2 * D_QK))
            ka_s.append(k_s.reshape(bs, t_new, H_A, 2 * D_QK))
            va_s.append(v_s.reshape(bs, t_new, H_A, 2 * D_QK))
            last = s_len - ((s_len - 1) // CHUNK_B) * CHUNK_B
            vb_p.append(g_p[s_len - last:].reshape(1, last, w))
            vb_s.append(g_s.reshape(bs, t_new, w))

            o_p = _attn_a_prompt(zb_p, bias_a_prompt, lambda_qk[e], subln_gain[e], li, tq=tq_a)
            o_s = _attn_a_sample(zb_s.reshape(bs, t_new, 3 * w), cache_k_a, cache_v_a, page_table, e,
                                 bias_a_past, bias_a_new, lambda_qk[e], subln_gain[e], li)
            gate_p = _gate_prompt(u_p, g_p, w_spatial[e], b_spatial[e])
            gate_s = _gate_sample(u_s.reshape(bs, t_new, w), g_s.reshape(bs, t_new, w), w_spatial[e], b_spatial[e])
            w_out = w_out_ab[e].astype(BF16)
            xp = _out_proj(o_p, gate_p, w_out, xp, ng[1, 1])
            xs = _out_proj(o_s.reshape(n_s, w), gate_s.reshape(n_s, w), w_out, xs, ng[1, 1])
        else:
            o = li // 2
            w_in = w_in_c[o].astype(BF16)
            zf_p = _proj_c(xp, ng[1, 0], w_in)
            zf_s = _proj_c(xs, ng[1, 0], w_in)
            keep = min(max(wd for wd, _ in DILATED_BRANCHES), s_len)
            kc_p.append(zf_p[s_len - keep:, d:2 * d].reshape(1, keep, H_C, D_C))
            vc_p.append(zf_p[s_len - keep:, 2 * d:].reshape(1, keep, H_C, D_C))
            kc_s.append(zf_s[:, d:2 * d].reshape(bs, t_new, H_C, D_C))
            vc_s.append(zf_s[:, 2 * d:].reshape(bs, t_new, H_C, D_C))
            o_p = _attn_c_prompt(zf_p, s_len, bias_c_prompt)
            o_s = _attn_c_sample(zf_s.astype(BF16).reshape(bs, t_new, 3 * d), cache_k_c, cache_v_c, o,
                                 bias_c_past, bias_c_new).reshape(n_s, d)
            w_out = w_out_c[o].astype(BF16)
            xp = _out_proj(o_p, o_p, w_out, xp, ng[1, 1], a_blk=0, b_blk=1)
            xs = _out_proj(o_s, o_s, w_out, xs, ng[1, 1], a_blk=0, b_blk=1)
        xp, xs = half_ffn(xp, xs, 1)

    y_prompt = xp.reshape(1, s_len, d)
    y_sample = xs.reshape(bs, t_new, d)
    return (y_prompt, y_sample,
            jnp.stack(ka_p, axis=1), jnp.stack(va_p, axis=1), jnp.stack(ka_s, axis=1), jnp.stack(va_s, axis=1),
            jnp.stack(vb_p, axis=0), jnp.stack(vb_s, axis=0),
            jnp.stack(kc_p, axis=0), jnp.stack(vc_p, axis=0), jnp.stack(kc_s, axis=0), jnp.stack(vc_s, axis=0))
```

```python
import functools
import math

import numpy as np
import jax
import jax.numpy as jnp
from jax import lax
from jax.experimental import pallas as pl
from jax.experimental.pallas import tpu as pltpu

F32 = jnp.float32
BF16 = jnp.bfloat16

EPS = 1e-6
NEG_INF = -1e30
LOG2E = math.log2(math.e)
LANE = 128
VMEM_LIMIT = 56 * 1024 * 1024

H_A = 8
D_QK = 64
G_B = 8
C_B = 128
CHUNK_B = 128
H_C = 16
D_C = 128
DILATED_BRANCHES = ((128, 1), (512, 4), (2048, 16))
N_BUCKETS = 32
MAX_DISTANCE = 128
MASKED = N_BUCKETS


def _params(*sem):
    return pltpu.CompilerParams(dimension_semantics=sem, vmem_limit_bytes=VMEM_LIMIT)


def _rms(x, g):
    return x * lax.rsqrt(jnp.mean(x * x, axis=-1, keepdims=True) + EPS) * g


def _gelu(x):
    return 0.5 * x * (1.0 + lax.erf(x * np.float32(math.sqrt(0.5))))


def _dot(a, b):
    return jnp.dot(a, b, preferred_element_type=F32)


def _dot_nt(a, b):
    return lax.dot_general(a, b, (((1,), (1,)), ((), ())), preferred_element_type=F32)


def _dot_tn(a, b):
    return lax.dot_general(a, b, (((0,), (0,)), ((), ())), preferred_element_type=F32)


def _lambda_init(layer):
    return 0.8 - 0.6 * math.exp(-0.3 * layer)


def _diff_lambda(lam_ref, layer):
    lp = lam_ref[...]
    a = jnp.sum(lp[0:1] * lp[1:2], axis=-1, keepdims=True)
    b = jnp.sum(lp[2:3] * lp[3:4], axis=-1, keepdims=True)
    return jnp.exp(a) - jnp.exp(b) + np.float32(_lambda_init(layer))


def _ffn_kernel(x_ref, gpre_ref, gpost_ref, wg_ref, wu_ref, wd_ref, o_ref, h_ref):
    f = pl.program_id(1)

    @pl.when(f == 0)
    def _():
        h_ref[...] = _rms(x_ref[...], gpre_ref[...]).astype(BF16)
        o_ref[...] = jnp.zeros_like(o_ref)

    h = h_ref[...]
    a = _dot(h, wg_ref[...])
    b = _dot(h, wu_ref[...])
    s = (a * jax.nn.sigmoid(a)) * b
    o_ref[...] += _dot(s.astype(BF16), wd_ref[...])

    @pl.when(f == pl.num_programs(1) - 1)
    def _():
        o_ref[...] = x_ref[...] + 0.5 * _rms(o_ref[...], gpost_ref[...])


def _ffn_half(x, g_pre, g_post, wg, wu, wd, *, tm=512, tf=512):
    n, d = x.shape
    f = wg.shape[1]
    return pl.pallas_call(
        _ffn_kernel,
        grid=(n // tm, f // tf),
        in_specs=[
            pl.BlockSpec((tm, d), lambda i, j: (i, 0)),
            pl.BlockSpec((1, d), lambda i, j: (0, 0)),
            pl.BlockSpec((1, d), lambda i, j: (0, 0)),
            pl.BlockSpec((d, tf), lambda i, j: (0, j)),
            pl.BlockSpec((d, tf), lambda i, j: (0, j)),
            pl.BlockSpec((tf, d), lambda i, j: (j, 0)),
        ],
        out_specs=pl.BlockSpec((tm, d), lambda i, j: (i, 0)),
        out_shape=jax.ShapeDtypeStruct((n, d), F32),
        scratch_shapes=[pltpu.VMEM((tm, d), BF16)],
        compiler_params=_params("parallel", "arbitrary"),
        name="ffn_half",
    )(x, g_pre.reshape(1, d), g_post.reshape(1, d), wg, wu, wd)


def _ffn_cast_kernel(x_ref, gpre_ref, gpost_ref, wg_ref, wu_ref, wd_ref, o_ref, wgb_ref, wub_ref, wdb_ref, h_ref):
    wgb_ref[...] = wg_ref[...].astype(BF16)
    wub_ref[...] = wu_ref[...].astype(BF16)
    wdb_ref[...] = wd_ref[...].astype(BF16)
    _ffn_kernel(x_ref, gpre_ref, gpost_ref, wgb_ref, wub_ref, wdb_ref, o_ref, h_ref)


def _ffn_half_cast(x, g_pre, g_post, wg, wu, wd, li, hj, *, tf=256):
    n, d = x.shape
    f = wg.shape[-1]
    once = pl.Buffered(1)
    out, wgb, wub, wdb = pl.pallas_call(
        _ffn_cast_kernel,
        grid=(1, f // tf),
        in_specs=[
            pl.BlockSpec((n, d), lambda i, j: (0, 0), pipeline_mode=once),
            pl.BlockSpec((1, d), lambda i, j: (0, 0)),
            pl.BlockSpec((1, d), lambda i, j: (0, 0)),
            pl.BlockSpec((None, None, d, tf), lambda i, j: (li, hj, 0, j)),
            pl.BlockSpec((None, None, d, tf), lambda i, j: (li, hj, 0, j)),
            pl.BlockSpec((None, None, tf, d), lambda i, j: (li, hj, j, 0)),
        ],
        out_specs=[
            pl.BlockSpec((n, d), lambda i, j: (0, 0)),
            pl.BlockSpec((d, tf), lambda i, j: (0, j)),
            pl.BlockSpec((d, tf), lambda i, j: (0, j)),
            pl.BlockSpec((tf, d), lambda i, j: (j, 0)),
        ],
        out_shape=[jax.ShapeDtypeStruct((n, d), F32), jax.ShapeDtypeStruct((d, f), BF16),
                   jax.ShapeDtypeStruct((d, f), BF16), jax.ShapeDtypeStruct((f, d), BF16)],
        scratch_shapes=[pltpu.VMEM((n, d), BF16)],
        compiler_params=_params("arbitrary", "arbitrary"),
        name="ffn_half_cast",
    )(x, g_pre.reshape(1, d), g_post.reshape(1, d), wg, wu, wd)
    return out, (wgb, wub, wdb)


def _proj_ab_kernel(x_ref, g_ref, w_ref, lng_ref, lnb_ref, k_ref, v_ref, u_ref, gv_ref, zb_ref, h_ref):
    j = pl.program_id(1)

    @pl.when(j == 0)
    def _():
        h_ref[...] = _rms(x_ref[...], g_ref[...]).astype(BF16)

    z = _dot(h_ref[...], w_ref[...])

    @pl.when(j == 0)
    def _():
        zb_ref[...] = (z * np.float32(D_QK ** -0.5 * LOG2E)).astype(BF16)

    @pl.when(j == 1)
    def _():
        k_ref[...] = z
        zb_ref[...] = z.astype(BF16)

    @pl.when(j == 2)
    def _():
        v_ref[...] = z
        zb_ref[...] = z.astype(BF16)

    @pl.when(j == 3)
    def _():
        u_ref[...] = _gelu(z)

    @pl.when(j == 4)
    def _():
        a = _gelu(z)
        c = a - jnp.mean(a, axis=-1, keepdims=True)
        y = c * lax.rsqrt(jnp.mean(c * c, axis=-1, keepdims=True) + EPS)
        gv_ref[...] = y * lng_ref[...] + lnb_ref[...]


def _proj_ab(x, g_pre, w_in, ln_g, ln_b, *, tm=512):
    n, d = x.shape
    w = w_in.shape[1] // 5
    row = pl.BlockSpec((tm, w), lambda i, j: (i, 0))
    return pl.pallas_call(
        _proj_ab_kernel,
        grid=(n // tm, 5),
        in_specs=[
            pl.BlockSpec((tm, d), lambda i, j: (i, 0)),
            pl.BlockSpec((1, d), lambda i, j: (0, 0)),
            pl.BlockSpec((d, w), lambda i, j: (0, j)),
            pl.BlockSpec((1, w), lambda i, j: (0, 0)),
            pl.BlockSpec((1, w), lambda i, j: (0, 0)),
        ],
        out_specs=[row, row, row, row, pl.BlockSpec((tm, w), lambda i, j: (i, jnp.minimum(j, 2)))],
        out_shape=[jax.ShapeDtypeStruct((n, w), F32)] * 4 + [jax.ShapeDtypeStruct((n, 3 * w), BF16)],
        scratch_shapes=[pltpu.VMEM((tm, d), BF16)],
        compiler_params=_params("parallel", "arbitrary"),
        name="proj_ab",
    )(x, g_pre.reshape(1, d), w_in, ln_g.reshape(1, w), ln_b.reshape(1, w))


def _proj_c_kernel(x_ref, g_ref, w_ref, zf_ref, h_ref):
    j = pl.program_id(1)

    @pl.when(j == 0)
    def _():
        h_ref[...] = _rms(x_ref[...], g_ref[...]).astype(BF16)

    z = _dot(h_ref[...], w_ref[...])

    @pl.when(j == 0)
    def _():
        zf_ref[...] = z * np.float32(D_C ** -0.5)

    @pl.when(j > 0)
    def _():
        zf_ref[...] = z


def _proj_c(x, g_pre, w_in, *, tm=512):
    n, d = x.shape
    w = w_in.shape[1] // 3
    return pl.pallas_call(
        _proj_c_kernel,
        grid=(n // tm, 3),
        in_specs=[
            pl.BlockSpec((tm, d), lambda i, j: (i, 0)),
            pl.BlockSpec((1, d), lambda i, j: (0, 0)),
            pl.BlockSpec((d, w), lambda i, j: (0, j)),
        ],
        out_specs=pl.BlockSpec((tm, w), lambda i, j: (i, j)),
        out_shape=jax.ShapeDtypeStruct((n, 3 * w), F32),
        scratch_shapes=[pltpu.VMEM((tm, d), BF16)],
        compiler_params=_params("parallel", "arbitrary"),
        name="proj_c",
    )(x, g_pre.reshape(1, d), w_in)


def _out_proj_kernel(a_ref, b_ref, wa_ref, wb_ref, x_ref, g_ref, o_ref):
    y = _dot(a_ref[...].astype(BF16), wa_ref[...]) + _dot(b_ref[...].astype(BF16), wb_ref[...])
    o_ref[...] = x_ref[...] + _rms(y, g_ref[...])


def _out_proj(a, b, w_out, x, g_post, *, a_blk=0, b_blk=0, tm=512):
    n, d = x.shape
    kh = w_out.shape[0] // 2
    return pl.pallas_call(
        _out_proj_kernel,
        grid=(n // tm,),
        in_specs=[
            pl.BlockSpec((tm, kh), lambda i: (i, a_blk)),
            pl.BlockSpec((tm, kh), lambda i: (i, b_blk)),
            pl.BlockSpec((kh, d), lambda i: (0, 0)),
            pl.BlockSpec((kh, d), lambda i: (1, 0)),
            pl.BlockSpec((tm, d), lambda i: (i, 0)),
            pl.BlockSpec((1, d), lambda i: (0, 0)),
        ],
        out_specs=pl.BlockSpec((tm, d), lambda i: (i, 0)),
        out_shape=jax.ShapeDtypeStruct((n, d), F32),
        compiler_params=_params("parallel"),
        name="out_proj",
    )(a, b, w_out, w_out, x, g_post.reshape(1, d))


def _bucket_np(dist):
    max_exact = N_BUCKETS // 2
    d = np.maximum(dist, 0)
    ratio = np.log(np.maximum(d, 1).astype(np.float32) / np.float32(max_exact)) / np.float32(math.log(MAX_DISTANCE / max_exact))
    large = np.minimum(max_exact + (ratio * (N_BUCKETS - max_exact)).astype(np.int32), N_BUCKETS - 1)
    return np.where(d < max_exact, d, large).astype(np.int32)


def _bias_expand_kernel(idx_ref, tab_ref, o_ref):
    idx = idx_ref[...]
    val = jnp.zeros(idx.shape, F32)
    for b in range(N_BUCKETS + 1):
        val = jnp.where(idx == b, tab_ref[b:b + 1, :], val)
    o_ref[...] = val


def _bias_expand(idx, tab, *, tr=256):
    gi, r, c = idx.shape
    g = tab.shape[0]
    tr = min(tr, r)
    return pl.pallas_call(
        _bias_expand_kernel,
        grid=(g, r // tr),
        in_specs=[
            pl.BlockSpec((None, tr, c), (lambda a, i: (a, i, 0)) if gi == g else (lambda a, i: (0, i, 0))),
            pl.BlockSpec((None, N_BUCKETS + 1, c), lambda a, i: (a, 0, 0)),
        ],
        out_specs=pl.BlockSpec((None, tr, c), lambda a, i: (a, i, 0)),
        out_shape=jax.ShapeDtypeStruct((g, r, c), F32),
        compiler_params=_params("parallel", "parallel"),
        name="bias_expand",
    )(jnp.asarray(idx), tab)


def _head_table(rel_bias, heads_of_col):
    t = rel_bias.astype(F32)[:, np.asarray(heads_of_col)]
    return jnp.concatenate([t, jnp.full((1, t.shape[1]), NEG_INF, F32)], axis=0)


def _attn_a_prompt_kernel(q_ref, k_ref, v_ref, bias_ref, lam_ref, sg_ref, o_ref, m_ref, l_ref, acc_ref, *, tq, layer):
    qi = pl.program_id(1)
    q = q_ref[...]
    lane = lax.broadcasted_iota(jnp.int32, q.shape, 1)
    zero = jnp.zeros_like(q)
    qq = jnp.concatenate([jnp.where(lane < D_QK, q, zero), jnp.where(lane >= D_QK, q, zero)], axis=0)
    m_ref[...] = jnp.full(m_ref.shape, NEG_INF, F32)
    l_ref[...] = jnp.zeros(l_ref.shape, F32)
    acc_ref[...] = jnp.zeros(acc_ref.shape, F32)

    def tile(j, bias_tile):
        off = pl.multiple_of(j * tq, tq)
        s = _dot_nt(qq, k_ref[pl.ds(off, tq), :])
        if bias_tile is not None:
            b = bias_ref[bias_tile]
            s = s + jnp.concatenate([b, b], axis=0)
        st = [s[:, c * LANE:(c + 1) * LANE] for c in range(tq // LANE)]
        m_prev = m_ref[...]
        m_new = jnp.maximum(m_prev, jnp.max(functools.reduce(jnp.maximum, st), axis=-1, keepdims=True))
        alpha = jnp.exp2(m_prev - m_new)
        ps = [jnp.exp2(t - m_new) for t in st]
        l_ref[...] = alpha * l_ref[...] + functools.reduce(lambda a, b: a + b, ps)
        p = jnp.concatenate([x.astype(BF16) for x in ps], axis=1)
        acc_ref[...] = alpha * acc_ref[...] + _dot(p, v_ref[pl.ds(off, tq), :])
        m_ref[...] = m_new

    def far_body(j, carry):
        tile(j, None)
        return carry

    lax.fori_loop(0, qi - 1, far_body, 0)

    @pl.when(qi >= 1)
    def _():
        tile(qi - 1, 1)

    tile(qi, 0)
    lam = _diff_lambda(lam_ref, layer)
    o = acc_ref[...] / jnp.sum(l_ref[...], axis=-1, keepdims=True)
    o = o[:tq] - lam * o[tq:]
    o_ref[...] = (_rms(o, sg_ref[...]) * np.float32(1.0 - _lambda_init(layer))).astype(o_ref.dtype)


def _attn_a_prompt(zb, bias_tiles, lam_p, subln_g, layer, *, tq=512):
    s_len = zb.shape[0]
    w = zb.shape[1] // 3
    hw = w // H_A
    return pl.pallas_call(
        functools.partial(_attn_a_prompt_kernel, tq=tq, layer=layer),
        grid=(H_A, s_len // tq),
        in_specs=[
            pl.BlockSpec((tq, hw), lambda h, i: (i, h)),
            pl.BlockSpec((s_len, hw), lambda h, i: (0, H_A + h)),
            pl.BlockSpec((s_len, hw), lambda h, i: (0, 2 * H_A + h)),
            pl.BlockSpec((None, 2, tq, tq), lambda h, i: (h, 0, 0, 0)),
            pl.BlockSpec((4, D_QK), lambda h, i: (0, 0)),
            pl.BlockSpec((1, hw), lambda h, i: (0, 0)),
        ],
        out_specs=pl.BlockSpec((tq, hw), lambda h, i: (i, h)),
        out_shape=jax.ShapeDtypeStruct((s_len, w), BF16),
        scratch_shapes=[pltpu.VMEM((2 * tq, LANE), F32), pltpu.VMEM((2 * tq, LANE), F32), pltpu.VMEM((2 * tq, hw), F32)],
        compiler_params=_params("parallel", "arbitrary"),
        name="attn_a_prompt",
    )(zb, zb, zb, bias_tiles, lam_p, subln_g.reshape(1, hw))


def _bias_tiles_a_prompt(rel_bias, tq):
    assert _bucket_np(np.array([tq + 1]))[0] == N_BUCKETS - 1
    i = np.arange(tq)[:, None]
    j = np.arange(tq)[None, :]
    idx = []
    for delta in (0, tq):
        dist = delta + i - j
        idx.append(np.where(dist >= 0, _bucket_np(dist), MASKED))
    idx = np.stack(idx).reshape(1, 2 * tq, tq).astype(np.int32)
    shifted = (rel_bias - rel_bias[N_BUCKETS - 1:]) * np.float32(LOG2E)
    tab = jnp.stack([_head_table(shifted, np.full(tq, h)) for h in range(H_A)])
    return _bias_expand(idx, tab, tr=min(tq, 256)).reshape(H_A, 2, tq, tq)


def _block_diag_queries(q, group_width):
    t = q.shape[0]
    tiled = jnp.concatenate([q] * (LANE // t), axis=0).astype(F32)
    row_g = lax.broadcasted_iota(jnp.int32, tiled.shape, 0) // t
    col_g = lax.broadcasted_iota(jnp.int32, tiled.shape, 1) // group_width
    return jnp.where(row_g == col_g, tiled, 0.0).T.astype(BF16)


def _attn_a_sample_kernel(pt_ref, q_ref, kn_ref, vn_ref, bp_ref, bn_ref, lam_ref, sg_ref, *rest, n_pages, page, layer):
    k_pages = rest[:n_pages]
    v_pages = rest[n_pages:2 * n_pages]
    o_ref = rest[2 * n_pages]
    s_ref, p_ref = rest[2 * n_pages + 1:]
    wq = _block_diag_queries(q_ref[...], D_QK)

    def heads_to_lanes(ref):
        return jnp.concatenate([ref[pl.ds(h, page, stride=H_A), :] for h in range(H_A)], axis=1).astype(BF16)

    for p in range(n_pages):
        s_ref[p * page:(p + 1) * page, :] = _dot(heads_to_lanes(k_pages[p]), wq) + bp_ref[p * page:(p + 1) * page, :]
    s_new = _dot(kn_ref[...], wq) + bn_ref[...]
    s_past = s_ref[...]
    m = jnp.maximum(jnp.max(s_past, axis=0, keepdims=True), jnp.max(s_new, axis=0, keepdims=True))
    e_past = jnp.exp2(s_past - m)
    e_new = jnp.exp2(s_new - m)
    den = jnp.sum(e_past, axis=0, keepdims=True) + jnp.sum(e_new, axis=0, keepdims=True)
    col = lax.broadcasted_iota(jnp.int32, den.shape, 1)
    t_new = kn_ref.shape[0]
    lam = _diff_lambda(lam_ref, layer)
    scale = jnp.where((col // t_new) % 2 == 0, 1.0, -lam) / den
    p_ref[...] = (e_past * scale).astype(BF16)
    acc = _dot_tn((e_new * scale).astype(BF16), vn_ref[...])
    for p in range(n_pages):
        acc = acc + _dot_tn(p_ref[p * page:(p + 1) * page, :], heads_to_lanes(v_pages[p]))
    for h in range(H_A):
        r0 = 2 * h * t_new
        o = acc[r0:r0 + t_new, h * LANE:(h + 1) * LANE] + acc[r0 + t_new:r0 + 2 * t_new, h * LANE:(h + 1) * LANE]
        o_ref[:, h * LANE:(h + 1) * LANE] = (_rms(o, sg_ref[...]) * np.float32(1.0 - _lambda_init(layer))).astype(o_ref.dtype)


def _attn_a_sample(zb_s, cache_k, cache_v, page_table, e, bias_past, bias_new, lam_p, subln_g, layer):
    bsz, t, w3 = zb_s.shape
    w = w3 // 3
    n_pages = page_table.shape[1]
    page = cache_k.shape[2]
    assert 2 * H_A * t == LANE

    cache_k = cache_k.reshape(cache_k.shape[0], cache_k.shape[1], page * H_A, LANE)
    cache_v = cache_v.reshape(cache_v.shape[0], cache_v.shape[1], page * H_A, LANE)

    def page_spec(p):
        return pl.BlockSpec((None, None, page * H_A, LANE), lambda b, pt: (pt[b * n_pages + p], e, 0, 0))

    grid_spec = pltpu.PrefetchScalarGridSpec(
        num_scalar_prefetch=1,
        grid=(bsz,),
        in_specs=[
            pl.BlockSpec((None, t, w), lambda b, pt: (b, 0, 0)),
            pl.BlockSpec((None, t, w), lambda b, pt: (b, 0, 1)),
            pl.BlockSpec((None, t, w), lambda b, pt: (b, 0, 2)),
            pl.BlockSpec((n_pages * page, LANE), lambda b, pt: (0, 0)),
            pl.BlockSpec((t, LANE), lambda b, pt: (0, 0)),
            pl.BlockSpec((4, D_QK), lambda b, pt: (0, 0)),
            pl.BlockSpec((1, LANE), lambda b, pt: (0, 0)),
        ] + [page_spec(p) for p in range(n_pages)] + [page_spec(p) for p in range(n_pages)],
        out_specs=pl.BlockSpec((None, t, w), lambda b, pt: (b, 0, 0)),
        scratch_shapes=[pltpu.VMEM((n_pages * page, LANE), F32), pltpu.VMEM((n_pages * page, LANE), BF16)],
    )
    return pl.pallas_call(
        functools.partial(_attn_a_sample_kernel, n_pages=n_pages, page=page, layer=layer),
        grid_spec=grid_spec,
        out_shape=jax.ShapeDtypeStruct((bsz, t, w), BF16),
        compiler_params=_params("arbitrary"),
        name="attn_a_sample",
    )(page_table.reshape(-1), zb_s, zb_s, zb_s, bias_past, bias_new, lam_p, subln_g.reshape(1, LANE),
      *([cache_k] * n_pages), *([cache_v] * n_pages))


def _bias_a_sample(rel_bias, past, t):
    kpos = np.arange(past + t)[:, None]
    col = np.arange(LANE)[None, :]
    dist = past + (col % t) - kpos
    idx = np.where(dist >= 0, _bucket_np(dist), MASKED).astype(np.int32)
    pad = (-idx.shape[0]) % 8
    idx = np.pad(idx, ((0, pad), (0, 0)), constant_values=MASKED)[None]
    tab = _head_table(rel_bias * np.float32(LOG2E), np.arange(LANE) // (2 * t))[None]
    out = _bias_expand(idx, tab, tr=idx.shape[1])[0]
    return out[:past], out[past:past + t]


def _gate_prompt_kernel(u_ref, g_ref, w_ref, b_ref, o_ref, *, chunks):
    row = lax.broadcasted_iota(jnp.int32, (CHUNK_B, CHUNK_B), 0)
    col = lax.broadcasted_iota(jnp.int32, (CHUNK_B, CHUNK_B), 1)
    for gi in range(G_B):
        wg = jnp.where(row >= col, w_ref[gi], 0.0).astype(BF16)
        bg = b_ref[:, gi:gi + 1]
        for c in range(chunks):
            rs = slice(c * CHUNK_B, (c + 1) * CHUNK_B)
            cs = slice(gi * C_B, (gi + 1) * C_B)
            mixed = _dot(wg, g_ref[rs, cs].astype(BF16)) + bg
            o_ref[rs, cs] = (u_ref[rs, cs] * mixed).astype(o_ref.dtype)


def _gate_prompt(u, g, w_s, b_s, *, chunks=4):
    s_len, w = u.shape
    tm = chunks * CHUNK_B
    return pl.pallas_call(
        functools.partial(_gate_prompt_kernel, chunks=chunks),
        grid=(s_len // tm,),
        in_specs=[
            pl.BlockSpec((tm, w), lambda i: (i, 0)),
            pl.BlockSpec((tm, w), lambda i: (i, 0)),
            pl.BlockSpec((G_B, CHUNK_B, CHUNK_B), lambda i: (0, 0, 0)),
            pl.BlockSpec((CHUNK_B, G_B), lambda i: (0, 0)),
        ],
        out_specs=pl.BlockSpec((tm, w), lambda i: (i, 0)),
        out_shape=jax.ShapeDtypeStruct((s_len, w), BF16),
        compiler_params=_params("parallel"),
        name="gate_prompt",
    )(u, g, w_s, b_s.T)


def _gate_sample_kernel(u_ref, g_ref, w_ref, b_ref, o_ref):
    t = u_ref.shape[1]
    g = g_ref[...]
    mixed = jnp.zeros(g.shape, F32) + b_ref[...][None]
    for s in range(t):
        mixed = mixed + w_ref[:, s, :][None] * g[:, s:s + 1, :]
    o_ref[...] = (u_ref[...] * mixed).astype(o_ref.dtype)


def _gate_sample(u, g, w_s, b_s, *, tb=32):
    bsz, t, w = u.shape
    causal = np.tril(np.ones((t, t), bool))
    wt = jnp.where(jnp.asarray(causal)[None], w_s[:, :t, :t], 0.0)
    wt = jnp.repeat(jnp.transpose(wt, (1, 2, 0)), C_B, axis=2)
    bt = jnp.repeat(b_s[:, :t].T, C_B, axis=1)
    return pl.pallas_call(
        _gate_sample_kernel,
        grid=(bsz // tb,),
        in_specs=[
            pl.BlockSpec((tb, t, w), lambda i: (i, 0, 0)),
            pl.BlockSpec((tb, t, w), lambda i: (i, 0, 0)),
            pl.BlockSpec((t, t, w), lambda i: (0, 0, 0)),
            pl.BlockSpec((t, w), lambda i: (0, 0)),
        ],
        out_specs=pl.BlockSpec((tb, t, w), lambda i: (i, 0, 0)),
        out_shape=jax.ShapeDtypeStruct((bsz, t, w), BF16),
        compiler_params=_params("parallel"),
        name="gate_sample",
    )(u, g, wt, bt)


def _attn_c_prompt_kernel(q_ref, kp_ref, kc_ref, vp_ref, vc_ref, bias_ref, o_ref, ob_ref, lb_ref, *, sb_len):
    sb = pl.program_id(0)

    def rows(ref, start, n, dil):
        return ref[pl.ds(start, n), :] if dil == 1 else ref[pl.ds(start, n, stride=dil), :]

    for bi, (window, dil) in enumerate(DILATED_BRANCHES):
        nstep = window // dil
        span = dil * nstep
        col = lax.broadcasted_iota(jnp.int32, (1, 2 * nstep), 1)
        no_prev = jnp.where(jnp.logical_and(sb == 0, col < nstep), np.float32(NEG_INF), np.float32(0.0))
        for blk in range(sb_len // span):
            base = blk * span
            for r in range(dil):
                q = rows(q_ref, base + r, nstep, dil).astype(BF16)
                if blk == 0:
                    k_prev = rows(kp_ref, sb_len - span + r, nstep, dil)
                    v_prev = rows(vp_ref, sb_len - span + r, nstep, dil)
                else:
                    k_prev = rows(kc_ref, base - span + r, nstep, dil)
                    v_prev = rows(vc_ref, base - span + r, nstep, dil)
                k = jnp.concatenate([k_prev, rows(kc_ref, base + r, nstep, dil)], axis=0).astype(BF16)
                v = jnp.concatenate([v_prev, rows(vc_ref, base + r, nstep, dil)], axis=0).astype(BF16)
                s = _dot_nt(q, k) + bias_ref[bi]
                if blk == 0:
                    s = s + no_prev
                m = jnp.max(s, axis=-1, keepdims=True)
                e = jnp.exp(s - m)
                den = jnp.sum(e, axis=-1, keepdims=True)
                o = _dot(e.astype(BF16), v) / den
                lse = jnp.broadcast_to(m + jnp.log(den), o.shape)
                if dil == 1:
                    ob_ref[bi, pl.ds(base, nstep), :] = o
                    lb_ref[bi, pl.ds(base, nstep), :] = lse
                else:
                    ob_ref[bi, pl.ds(base + r, nstep, stride=dil), :] = o
                    lb_ref[bi, pl.ds(base + r, nstep, stride=dil), :] = lse
    ls = [lb_ref[b] for b in range(len(DILATED_BRANCHES))]
    mm = functools.reduce(jnp.maximum, ls)
    es = [jnp.exp(l - mm) for l in ls]
    den = functools.reduce(lambda a, b: a + b, es)
    out = functools.reduce(lambda a, b: a + b, [(e / den) * ob_ref[b] for b, e in enumerate(es)])
    o_ref[...] = out.astype(o_ref.dtype)


def _attn_c_prompt(zf, s_len, bias):
    d = zf.shape[1] // 3
    nbr = len(DILATED_BRANCHES)
    sb_len = max(window for window, _ in DILATED_BRANCHES)
    assert s_len % sb_len == 0 and all(sb_len % window == 0 for window, _ in DILATED_BRANCHES)
    nstep = bias.shape[2]
    prev = lambda part: (lambda i, h: (jnp.maximum(i - 1, 0), part * H_C + h))
    cur = lambda part: (lambda i, h: (i, part * H_C + h))
    return pl.pallas_call(
        functools.partial(_attn_c_prompt_kernel, sb_len=sb_len),
        grid=(s_len // sb_len, H_C),
        in_specs=[
            pl.BlockSpec((sb_len, D_C), cur(0)),
            pl.BlockSpec((sb_len, D_C), prev(1)),
            pl.BlockSpec((sb_len, D_C), cur(1)),
            pl.BlockSpec((sb_len, D_C), prev(2)),
            pl.BlockSpec((sb_len, D_C), cur(2)),
            pl.BlockSpec((nbr, None, nstep, 2 * nstep), lambda i, h: (0, h, 0, 0)),
        ],
        out_specs=pl.BlockSpec((sb_len, D_C), lambda i, h: (i, h)),
        out_shape=jax.ShapeDtypeStruct((s_len, d), BF16),
        scratch_shapes=[pltpu.VMEM((nbr, sb_len, D_C), F32), pltpu.VMEM((nbr, sb_len, D_C), F32)],
        compiler_params=_params("parallel", "arbitrary"),
        name="attn_c_prompt",
    )(zf, zf, zf, zf, zf, bias)


def _bias_tiles_c_prompt(rel_bias):
    idx = []
    for window, dil in DILATED_BRANCHES:
        nstep = window // dil
        step = nstep + np.arange(nstep)[:, None] - np.arange(2 * nstep)[None, :]
        band = (step >= 0) & (step <= nstep)
        idx.append(np.where(band, _bucket_np(np.clip(step, 0, nstep) * dil), MASKED))
    nstep = idx[0].shape[0]
    assert all(i.shape == (nstep, 2 * nstep) for i in idx)
    nbr = len(idx)
    idx = np.stack(idx).reshape(1, nbr * nstep, 2 * nstep).astype(np.int32)
    tab = jnp.stack([_head_table(rel_bias, np.full(2 * nstep, h)) for h in range(H_C)])
    out = _bias_expand(idx, tab, tr=nstep)
    return jnp.swapaxes(out.reshape(H_C, nbr, nstep, 2 * nstep), 0, 1)


def _attn_c_sample_kernel(q_ref, kn_ref, vn_ref, bp_ref, bn_ref, kh_ref, kf_ref, vh_ref, vf_ref, o_ref,
                          s_ref, p_ref, *, n_half, n_full, ck):
    t_new = kn_ref.shape[0]
    hk = ck // 2
    kept = KEY_GROUP // 2
    gpc = ck // KEY_GROUP

    def full_to_lanes(ref, c):
        return jnp.concatenate([ref[pl.ds(c * ck * H_C + h, ck, stride=H_C), :] for h in range(H_C)],
                               axis=1).astype(BF16)

    def half_to_lanes(ref, c):
        return jnp.concatenate(
            [jnp.concatenate([ref[g, pl.ds(h, kept, stride=H_C), :] for g in range(c * gpc, (c + 1) * gpc)], axis=0)
             for h in range(H_C)], axis=1).astype(BF16)

    chunks = [(half_to_lanes, c, slice(c * hk, (c + 1) * hk)) for c in range(n_half)]
    chunks += [(full_to_lanes, c, slice(n_half * hk + c * ck, n_half * hk + (c + 1) * ck)) for c in range(n_full)]

    wq = _block_diag_queries(q_ref[...], D_C)
    for to_lanes, c, rs in chunks:
        ref = kh_ref if to_lanes is half_to_lanes else kf_ref
        s_ref[rs, :] = _dot(to_lanes(ref, c), wq)
    s_past = s_ref[...]
    s_new = _dot(kn_ref[...], wq)
    es, lses = [], []
    for b in range(bp_ref.shape[0]):
        sp = s_past + bp_ref[b]
        sn = s_new + bn_ref[b]
        m = jnp.maximum(jnp.max(sp, axis=0, keepdims=True), jnp.max(sn, axis=0, keepdims=True))
        ep = jnp.exp(sp - m)
        en = jnp.exp(sn - m)
        den = jnp.sum(ep, axis=0, keepdims=True) + jnp.sum(en, axis=0, keepdims=True)
        es.append((ep, en, den))
        lses.append(m + jnp.log(den))
    mm = functools.reduce(jnp.maximum, lses)
    ws = [jnp.exp(l - mm) for l in lses]
    wsum = functools.reduce(lambda a, b: a + b, ws)
    pp = jnp.zeros(s_past.shape, F32)
    pn = jnp.zeros(s_new.shape, F32)
    for (ep, en, den), w in zip(es, ws):
        coef = (w / wsum) / den
        pp = pp + coef * ep
        pn = pn + coef * en
    p_ref[...] = pp.astype(BF16)
    acc = _dot_tn(pn.astype(BF16), vn_ref[...])
    for to_lanes, c, rs in chunks:
        ref = vh_ref if to_lanes is half_to_lanes else vf_ref
        acc = acc + _dot_tn(p_ref[rs, :], to_lanes(ref, c))
    for h in range(H_C):
        o_ref[:, h * D_C:(h + 1) * D_C] = acc[h * t_new:(h + 1) * t_new, h * D_C:(h + 1) * D_C].astype(o_ref.dtype)


KEY_GROUP = max(dil for _, dil in DILATED_BRANCHES)
CK_C = 512


def _c_sample_plan(wlen, t, ck):
    dist = wlen + np.arange(t)[None, :] - np.arange(wlen)[:, None]
    need = np.zeros(wlen, bool)
    for window, dil in DILATED_BRANCHES:
        need |= ((dist % dil == 0) & (dist <= window)).any(axis=1)
    nck = wlen // ck
    first_half = (np.arange(wlen) % KEY_GROUP) < KEY_GROUP // 2
    n_half = 0
    while n_half < nck and not (need & ~first_half)[n_half * ck:(n_half + 1) * ck].any():
        n_half += 1
    keep = first_half | (np.arange(wlen) >= n_half * ck)
    return n_half, np.nonzero(keep)[0]


def _attn_c_sample(zb_s, cache_k, cache_v, o_idx, bias_past, bias_new, *, ck=CK_C):
    bsz, t, d3 = zb_s.shape
    d = d3 // 3
    wlen = cache_k.shape[2]
    nck = wlen // ck
    nbr = bias_past.shape[0]
    n_half, kidx = _c_sample_plan(wlen, t, ck)
    nrows = len(kidx)
    assert H_C * t == LANE and wlen % ck == 0 and ck % (2 * KEY_GROUP) == 0
    assert 1 <= n_half < nck and bias_past.shape[1] == nrows
    grp = KEY_GROUP * H_C
    full = lambda a: a.reshape(a.shape[0], bsz, wlen * H_C, D_C)
    half = lambda a: a.reshape(a.shape[0], bsz, wlen // KEY_GROUP, grp, D_C)
    n_full = nck - n_half
    assert n_half % n_full == 0
    half_spec = pl.BlockSpec((None, None, n_half * ck // KEY_GROUP, grp // 2, D_C), lambda b: (o_idx, b, 0, 0, 0))
    full_spec = pl.BlockSpec((None, None, n_full * ck * H_C, D_C), lambda b: (o_idx, b, n_half // n_full, 0))
    return pl.pallas_call(
        functools.partial(_attn_c_sample_kernel, n_half=n_half, n_full=n_full, ck=ck),
        grid=(bsz,),
        in_specs=[
            pl.BlockSpec((None, t, d), lambda b: (b, 0, 0)),
            pl.BlockSpec((None, t, d), lambda b: (b, 0, 1)),
            pl.BlockSpec((None, t, d), lambda b: (b, 0, 2)),
            pl.BlockSpec((nbr, nrows, LANE), lambda b: (0, 0, 0)),
            pl.BlockSpec((nbr, t, LANE), lambda b: (0, 0, 0)),
            half_spec, full_spec, half_spec, full_spec,
        ],
        out_specs=pl.BlockSpec((None, t, d), lambda b: (b, 0, 0)),
        out_shape=jax.ShapeDtypeStruct((bsz, t, d), BF16),
        scratch_shapes=[pltpu.VMEM((nrows, LANE), F32), pltpu.VMEM((nrows, LANE), BF16)],
        compiler_params=_params("parallel"),
        name="attn_c_sample",
    )(zb_s, zb_s, zb_s, bias_past, bias_new, half(cache_k), full(cache_k), half(cache_v), full(cache_v))


def _bias_c_sample(rel_bias, wlen, t):
    _, kidx = _c_sample_plan(wlen, t, CK_C)
    nrows = len(kidx)
    krow = np.concatenate([kidx, wlen + np.arange(t)])[:, None]
    col = np.arange(LANE)[None, :]
    dist = wlen + (col % t) - krow
    idx = []
    for window, dil in DILATED_BRANCHES:
        ok = (dist >= 0) & (dist % dil == 0) & (dist <= window)
        idx.append(np.where(ok, _bucket_np(dist), MASKED))
    idx = np.stack(idx).astype(np.int32)
    pad = (-idx.shape[1]) % 8
    idx = np.pad(idx, ((0, 0), (0, pad), (0, 0)), constant_values=MASKED)
    tab = jnp.broadcast_to(_head_table(rel_bias, np.arange(LANE) // t)[None], (idx.shape[0], N_BUCKETS + 1, LANE))
    out = _bias_expand(idx, tab, tr=idx.shape[1])
    return out[:, :nrows], out[:, nrows:nrows + t]


def kernel(x_prompt, x_sample, cache_k_a, cache_v_a, page_table, cache_k_c, cache_v_c, rel_bias, norm_gains, w_ffn_gate, w_ffn_up, w_ffn_down, w_in_ab, w_out_ab, lambda_qk, subln_gain, ln_v_gain, ln_v_bias, w_spatial, b_spatial, w_in_c, w_out_c):
    bp, s_len, d = x_prompt.shape
    bs, t_new, _ = x_sample.shape
    assert bp == 1
    n_s = bs * t_new
    depth = norm_gains.shape[0]
    xp = x_prompt.reshape(s_len, d)
    xs = x_sample.reshape(n_s, d)

    tq_a = 512
    past_a = page_table.shape[1] * cache_k_a.shape[2]
    bias_a_prompt = _bias_tiles_a_prompt(rel_bias, tq_a)
    bias_a_past, bias_a_new = _bias_a_sample(rel_bias, past_a, t_new)
    bias_c_prompt = _bias_tiles_c_prompt(rel_bias)
    bias_c_past, bias_c_new = _bias_c_sample(rel_bias, cache_k_c.shape[2], t_new)

    ka_p, va_p, ka_s, va_s, vb_p, vb_s = [], [], [], [], [], []
    kc_p, vc_p, kc_s, vc_s = [], [], [], []
    for li in range(depth):
        ng = norm_gains[li]

        def half_ffn(xp, xs, j):
            xs, wts = _ffn_half_cast(xs, ng[2 * j, 0], ng[2 * j, 1], w_ffn_gate, w_ffn_up, w_ffn_down, li, j)
            return _ffn_half(xp, ng[2 * j, 0], ng[2 * j, 1], *wts), xs

        xp, xs = half_ffn(xp, xs, 0)
        if li % 2 == 0:
            e = li // 2
            w = H_A * 2 * D_QK
            proj = (ng[1, 0], w_in_ab[e].astype(BF16), ln_v_gain[e], ln_v_bias[e])
            k_p, v_p, u_p, g_p, zb_p = _proj_ab(xp, *proj)
            k_s, v_s, u_s, g_s, zb_s = _proj_ab(xs, *proj)
            ka_p.append(k_p.reshape(1, s_len, H_A, 2 * D_QK))
            va_p.append(v_p.reshape(1, s_len, H_A, 2 * D_QK))
            ka_s.append(k_s.reshape(bs, t_new, H_A, 2 * D_QK))
            va_s.append(v_s.reshape(bs, t_new, H_A, 2 * D_QK))
            last = s_len - ((s_len - 1) // CHUNK_B) * CHUNK_B
            vb_p.append(g_p[s_len - last:].reshape(1, last, w))
            vb_s.append(g_s.reshape(bs, t_new, w))

            o_p = _attn_a_prompt(zb_p, bias_a_prompt, lambda_qk[e], subln_gain[e], li, tq=tq_a)
            o_s = _attn_a_sample(zb_s.reshape(bs, t_new, 3 * w), cache_k_a, cache_v_a, page_table, e,
                                 bias_a_past, bias_a_new, lambda_qk[e], subln_gain[e], li)
            gate_p = _gate_prompt(u_p, g_p, w_spatial[e], b_spatial[e])
            gate_s = _gate_sample(u_s.reshape(bs, t_new, w), g_s.reshape(bs, t_new, w), w_spatial[e], b_spatial[e])
            w_out = w_out_ab[e].astype(BF16)
            xp = _out_proj(o_p, gate_p, w_out, xp, ng[1, 1])
            xs = _out_proj(o_s.reshape(n_s, w), gate_s.reshape(n_s, w), w_out, xs, ng[1, 1])
        else:
            o = li // 2
            w_in = w_in_c[o].astype(BF16)
            zf_p = _proj_c(xp, ng[1, 0], w_in)
            zf_s = _proj_c(xs, ng[1, 0], w_in)
            keep = min(max(wd for wd, _ in DILATED_BRANCHES), s_len)
            kc_p.append(zf_p[s_len - keep:, d:2 * d].reshape(1, keep, H_C, D_C))
            vc_p.append(zf_p[s_len - keep:, 2 * d:].reshape(1, keep, H_C, D_C))
            kc_s.append(zf_s[:, d:2 * d].reshape(bs, t_new, H_C, D_C))
            vc_s.append(zf_s[:, 2 * d:].reshape(bs, t_new, H_C, D_C))
            o_p = _attn_c_prompt(zf_p, s_len, bias_c_prompt)
            o_s = _attn_c_sample(zf_s.astype(BF16).reshape(bs, t_new, 3 * d), cache_k_c, cache_v_c, o,
                                 bias_c_past, bias_c_new).reshape(n_s, d)
            w_out = w_out_c[o].astype(BF16)
            xp = _out_proj(o_p, o_p, w_out, xp, ng[1, 1], a_blk=0, b_blk=1)
            xs = _out_proj(o_s, o_s, w_out, xs, ng[1, 1], a_blk=0, b_blk=1)
        xp, xs = half_ffn(xp, xs, 1)

    y_prompt = xp.reshape(1, s_len, d)
    y_sample = xs.reshape(bs, t_new, d)
    return (y_prompt, y_sample,
            jnp.stack(ka_p, axis=1), jnp.stack(va_p, axis=1), jnp.stack(ka_s, axis=1), jnp.stack(va_s, axis=1),
            jnp.stack(vb_p, axis=0), jnp.stack(vb_s, axis=0),
            jnp.stack(kc_p, axis=0), jnp.stack(vc_p, axis=0), jnp.stack(kc_s, axis=0), jnp.stack(vc_s, axis=0))
```

```python
import functools
import math

import numpy as np
import jax
import jax.numpy as jnp
from jax import lax
from jax.experimental import pallas as pl
from jax.experimental.pallas import tpu as pltpu

F32 = jnp.float32
BF16 = jnp.bfloat16

EPS = 1e-6
NEG_INF = -1e30
LOG2E = math.log2(math.e)
LANE = 128
VMEM_LIMIT = 56 * 1024 * 1024

H_A = 8
D_QK = 64
G_B = 8
C_B = 128
CHUNK_B = 128
H_C = 16
D_C = 128
DILATED_BRANCHES = ((128, 1), (512, 4), (2048, 16))
N_BUCKETS = 32
MAX_DISTANCE = 128
MASKED = N_BUCKETS


def _params(*sem):
    return pltpu.CompilerParams(dimension_semantics=sem, vmem_limit_bytes=VMEM_LIMIT)


def _rms(x, g):
    return x * lax.rsqrt(jnp.mean(x * x, axis=-1, keepdims=True) + EPS) * g


def _gelu(x):
    return 0.5 * x * (1.0 + lax.erf(x * np.float32(math.sqrt(0.5))))


def _dot(a, b):
    return jnp.dot(a, b, preferred_element_type=F32)


def _dot_nt(a, b):
    return lax.dot_general(a, b, (((1,), (1,)), ((), ())), preferred_element_type=F32)


def _dot_tn(a, b):
    return lax.dot_general(a, b, (((0,), (0,)), ((), ())), preferred_element_type=F32)


def _lambda_init(layer):
    return 0.8 - 0.6 * math.exp(-0.3 * layer)


def _diff_lambda(lam_ref, layer):
    lp = lam_ref[...]
    a = jnp.sum(lp[0:1] * lp[1:2], axis=-1, keepdims=True)
    b = jnp.sum(lp[2:3] * lp[3:4], axis=-1, keepdims=True)
    return jnp.exp(a) - jnp.exp(b) + np.float32(_lambda_init(layer))


def _ffn_kernel(x_ref, gpre_ref, gpost_ref, wg_ref, wu_ref, wd_ref, o_ref, h_ref):
    f = pl.program_id(1)

    @pl.when(f == 0)
    def _():
        h_ref[...] = _rms(x_ref[...], gpre_ref[...]).astype(BF16)
        o_ref[...] = jnp.zeros_like(o_ref)

    h = h_ref[...]
    a = _dot(h, wg_ref[...])
    b = _dot(h, wu_ref[...])
    s = (a * jax.nn.sigmoid(a)) * b
    o_ref[...] += _dot(s.astype(BF16), wd_ref[...])

    @pl.when(f == pl.num_programs(1) - 1)
    def _():
        o_ref[...] = x_ref[...] + 0.5 * _rms(o_ref[...], gpost_ref[...])


def _ffn_half(x, g_pre, g_post, wg, wu, wd, *, tm=512, tf=512):
    n, d = x.shape
    f = wg.shape[1]
    return pl.pallas_call(
        _ffn_kernel,
        grid=(n // tm, f // tf),
        in_specs=[
            pl.BlockSpec((tm, d), lambda i, j: (i, 0)),
            pl.BlockSpec((1, d), lambda i, j: (0, 0)),
            pl.BlockSpec((1, d), lambda i, j: (0, 0)),
            pl.BlockSpec((d, tf), lambda i, j: (0, j)),
            pl.BlockSpec((d, tf), lambda i, j: (0, j)),
            pl.BlockSpec((tf, d), lambda i, j: (j, 0)),
        ],
        out_specs=pl.BlockSpec((tm, d), lambda i, j: (i, 0)),
        out_shape=jax.ShapeDtypeStruct((n, d), F32),
        scratch_shapes=[pltpu.VMEM((tm, d), BF16)],
        compiler_params=_params("parallel", "arbitrary"),
        name="ffn_half",
    )(x, g_pre.reshape(1, d), g_post.reshape(1, d), wg, wu, wd)


def _ffn_cast_kernel(x_ref, gpre_ref, gpost_ref, wg_ref, wu_ref, wd_ref, o_ref, wgb_ref, wub_ref, wdb_ref, h_ref):
    wgb_ref[...] = wg_ref[...].astype(BF16)
    wub_ref[...] = wu_ref[...].astype(BF16)
    wdb_ref[...] = wd_ref[...].astype(BF16)
    _ffn_kernel(x_ref, gpre_ref, gpost_ref, wgb_ref, wub_ref, wdb_ref, o_ref, h_ref)


def _ffn_half_cast(x, g_pre, g_post, wg, wu, wd, li, hj, *, tf=256):
    n, d = x.shape
    f = wg.shape[-1]
    once = pl.Buffered(1)
    out, wgb, wub, wdb = pl.pallas_call(
        _ffn_cast_kernel,
        grid=(1, f // tf),
        in_specs=[
            pl.BlockSpec((n, d), lambda i, j: (0, 0), pipeline_mode=once),
            pl.BlockSpec((1, d), lambda i, j: (0, 0)),
            pl.BlockSpec((1, d), lambda i, j: (0, 0)),
            pl.BlockSpec((None, None, d, tf), lambda i, j: (li, hj, 0, j)),
            pl.BlockSpec((None, None, d, tf), lambda i, j: (li, hj, 0, j)),
            pl.BlockSpec((None, None, tf, d), lambda i, j: (li, hj, j, 0)),
        ],
        out_specs=[
            pl.BlockSpec((n, d), lambda i, j: (0, 0)),
            pl.BlockSpec((d, tf), lambda i, j: (0, j)),
            pl.BlockSpec((d, tf), lambda i, j: (0, j)),
            pl.BlockSpec((tf, d), lambda i, j: (j, 0)),
        ],
        out_shape=[jax.ShapeDtypeStruct((n, d), F32), jax.ShapeDtypeStruct((d, f), BF16),
                   jax.ShapeDtypeStruct((d, f), BF16), jax.ShapeDtypeStruct((f, d), BF16)],
        scratch_shapes=[pltpu.VMEM((n, d), BF16)],
        compiler_params=_params("arbitrary", "arbitrary"),
        name="ffn_half_cast",
    )(x, g_pre.reshape(1, d), g_post.reshape(1, d), wg, wu, wd)
    return out, (wgb, wub, wdb)


def _proj_ab_kernel(x_ref, g_ref, w_ref, lng_ref, lnb_ref, k_ref, v_ref, u_ref, gv_ref, zb_ref, h_ref):
    j = pl.program_id(1)

    @pl.when(j == 0)
    def _():
        h_ref[...] = _rms(x_ref[...], g_ref[...]).astype(BF16)

    z = _dot(h_ref[...], w_ref[...])

    @pl.when(j == 0)
    def _():
        zb_ref[...] = (z * np.float32(D_QK ** -0.5 * LOG2E)).astype(BF16)

    @pl.when(j == 1)
    def _():
        k_ref[...] = z
        zb_ref[...] = z.astype(BF16)

    @pl.when(j == 2)
    def _():
        v_ref[...] = z
        zb_ref[...] = z.astype(BF16)

    @pl.when(j == 3)
    def _():
        u_ref[...] = _gelu(z)

    @pl.when(j == 4)
    def _():
        a = _gelu(z)
        c = a - jnp.mean(a, axis=-1, keepdims=True)
        y = c * lax.rsqrt(jnp.mean(c * c, axis=-1, keepdims=True) + EPS)
        gv_ref[...] = y * lng_ref[...] + lnb_ref[...]


def _proj_ab(x, g_pre, w_in, ln_g, ln_b, *, tm=512):
    n, d = x.shape
    w = w_in.shape[1] // 5
    row = pl.BlockSpec((tm, w), lambda i, j: (i, 0))
    return pl.pallas_call(
        _proj_ab_kernel,
        grid=(n // tm, 5),
        in_specs=[
            pl.BlockSpec((tm, d), lambda i, j: (i, 0)),
            pl.BlockSpec((1, d), lambda i, j: (0, 0)),
            pl.BlockSpec((d, w), lambda i, j: (0, j)),
            pl.BlockSpec((1, w), lambda i, j: (0, 0)),
            pl.BlockSpec((1, w), lambda i, j: (0, 0)),
        ],
        out_specs=[row, row, row, row, pl.BlockSpec((tm, w), lambda i, j: (i, jnp.minimum(j, 2)))],
        out_shape=[jax.ShapeDtypeStruct((n, w), F32)] * 4 + [jax.ShapeDtypeStruct((n, 3 * w), BF16)],
        scratch_shapes=[pltpu.VMEM((tm, d), BF16)],
        compiler_params=_params("parallel", "arbitrary"),
        name="proj_ab",
    )(x, g_pre.reshape(1, d), w_in, ln_g.reshape(1, w), ln_b.reshape(1, w))


def _proj_c_kernel(x_ref, g_ref, w_ref, zf_ref, h_ref):
    j = pl.program_id(1)

    @pl.when(j == 0)
    def _():
        h_ref[...] = _rms(x_ref[...], g_ref[...]).astype(BF16)

    z = _dot(h_ref[...], w_ref[...])

    @pl.when(j == 0)
    def _():
        zf_ref[...] = z * np.float32(D_C ** -0.5)

    @pl.when(j > 0)
    def _():
        zf_ref[...] = z


def _proj_c(x, g_pre, w_in, *, tm=512):
    n, d = x.shape
    w = w_in.shape[1] // 3
    return pl.pallas_call(
        _proj_c_kernel,
        grid=(n // tm, 3),
        in_specs=[
            pl.BlockSpec((tm, d), lambda i, j: (i, 0)),
            pl.BlockSpec((1, d), lambda i, j: (0, 0)),
            pl.BlockSpec((d, w), lambda i, j: (0, j)),
        ],
        out_specs=pl.BlockSpec((tm, w), lambda i, j: (i, j)),
        out_shape=jax.ShapeDtypeStruct((n, 3 * w), F32),
        scratch_shapes=[pltpu.VMEM((tm, d), BF16)],
        compiler_params=_params("parallel", "arbitrary"),
        name="proj_c",
    )(x, g_pre.reshape(1, d), w_in)


def _out_proj_kernel(a_ref, b_ref, wa_ref, wb_ref, x_ref, g_ref, o_ref):
    y = _dot(a_ref[...].astype(BF16), wa_ref[...]) + _dot(b_ref[...].astype(BF16), wb_ref[...])
    o_ref[...] = x_ref[...] + _rms(y, g_ref[...])


def _out_proj(a, b, w_out, x, g_post, *, a_blk=0, b_blk=0, tm=512):
    n, d = x.shape
    kh = w_out.shape[0] // 2
    return pl.pallas_call(
        _out_proj_kernel,
        grid=(n // tm,),
        in_specs=[
            pl.BlockSpec((tm, kh), lambda i: (i, a_blk)),
            pl.BlockSpec((tm, kh), lambda i: (i, b_blk)),
            pl.BlockSpec((kh, d), lambda i: (0, 0)),
            pl.BlockSpec((kh, d), lambda i: (1, 0)),
            pl.BlockSpec((tm, d), lambda i: (i, 0)),
            pl.BlockSpec((1, d), lambda i: (0, 0)),
        ],
        out_specs=pl.BlockSpec((tm, d), lambda i: (i, 0)),
        out_shape=jax.ShapeDtypeStruct((n, d), F32),
        compiler_params=_params("parallel"),
        name="out_proj",
    )(a, b, w_out, w_out, x, g_post.reshape(1, d))


def _bucket_np(dist):
    max_exact = N_BUCKETS // 2
    d = np.maximum(dist, 0)
    ratio = np.log(np.maximum(d, 1).astype(np.float32) / np.float32(max_exact)) / np.float32(math.log(MAX_DISTANCE / max_exact))
    large = np.minimum(max_exact + (ratio * (N_BUCKETS - max_exact)).astype(np.int32), N_BUCKETS - 1)
    return np.where(d < max_exact, d, large).astype(np.int32)


def _bias_expand_kernel(idx_ref, tab_ref, o_ref):
    idx = idx_ref[...]
    val = jnp.zeros(idx.shape, F32)
    for b in range(N_BUCKETS + 1):
        val = jnp.where(idx == b, tab_ref[b:b + 1, :], val)
    o_ref[...] = val


def _bias_expand(idx, tab, *, tr=256):
    gi, r, c = idx.shape
    g = tab.shape[0]
    tr = min(tr, r)
    return pl.pallas_call(
        _bias_expand_kernel,
        grid=(g, r // tr),
        in_specs=[
            pl.BlockSpec((None, tr, c), (lambda a, i: (a, i, 0)) if gi == g else (lambda a, i: (0, i, 0))),
            pl.BlockSpec((None, N_BUCKETS + 1, c), lambda a, i: (a, 0, 0)),
        ],
        out_specs=pl.BlockSpec((None, tr, c), lambda a, i: (a, i, 0)),
        out_shape=jax.ShapeDtypeStruct((g, r, c), F32),
        compiler_params=_params("parallel", "parallel"),
        name="bias_expand",
    )(jnp.asarray(idx), tab)


def _head_table(rel_bias, heads_of_col):
    t = rel_bias.astype(F32)[:, np.asarray(heads_of_col)]
    return jnp.concatenate([t, jnp.full((1, t.shape[1]), NEG_INF, F32)], axis=0)


def _attn_a_prompt_kernel(q_ref, k_ref, v_ref, bias_ref, lam_ref, sg_ref, o_ref, m_ref, l_ref, acc_ref, *, tq, layer):
    qi = pl.program_id(1)
    q = q_ref[...]
    lane = lax.broadcasted_iota(jnp.int32, q.shape, 1)
    zero = jnp.zeros_like(q)
    qq = jnp.concatenate([jnp.where(lane < D_QK, q, zero), jnp.where(lane >= D_QK, q, zero)], axis=0)
    m_ref[...] = jnp.full(m_ref.shape, NEG_INF, F32)
    l_ref[...] = jnp.zeros(l_ref.shape, F32)
    acc_ref[...] = jnp.zeros(acc_ref.shape, F32)

    def tile(j, bias_tile, nk=1):
        off = pl.multiple_of(j * tq, tq)
        tk = nk * tq
        s = _dot_nt(qq, k_ref[pl.ds(off, tk), :])
        if bias_tile is not None:
            b = bias_ref[bias_tile]
            s = s + jnp.concatenate([b, b], axis=0)
        st = [s[:, c * LANE:(c + 1) * LANE] for c in range(tk // LANE)]
        m_prev = m_ref[...]
        m_new = jnp.maximum(m_prev, jnp.max(functools.reduce(jnp.maximum, st), axis=-1, keepdims=True))
        alpha = jnp.exp2(m_prev - m_new)
        ps = [jnp.exp2(t - m_new) for t in st]
        l_ref[...] = alpha * l_ref[...] + functools.reduce(lambda a, b: a + b, ps)
        p = jnp.concatenate([x.astype(BF16) for x in ps], axis=1)
        acc_ref[...] = alpha * acc_ref[...] + _dot(p, v_ref[pl.ds(off, tk), :])
        m_ref[...] = m_new

    n_far = jnp.maximum(qi - 1, 0)

    def far_body(j2, carry):
        tile(2 * j2, None, nk=2)
        return carry

    lax.fori_loop(0, n_far // 2, far_body, 0)

    @pl.when(n_far % 2 == 1)
    def _():
        tile(n_far - 1, None)

    @pl.when(qi >= 1)
    def _():
        tile(qi - 1, 1)

    tile(qi, 0)
    lam = _diff_lambda(lam_ref, layer)
    o = acc_ref[...] / jnp.sum(l_ref[...], axis=-1, keepdims=True)
    o = o[:tq] - lam * o[tq:]
    o_ref[...] = (_rms(o, sg_ref[...]) * np.float32(1.0 - _lambda_init(layer))).astype(o_ref.dtype)


def _attn_a_prompt(zb, bias_tiles, lam_p, subln_g, layer, *, tq=512):
    s_len = zb.shape[0]
    w = zb.shape[1] // 3
    hw = w // H_A
    return pl.pallas_call(
        functools.partial(_attn_a_prompt_kernel, tq=tq, layer=layer),
        grid=(H_A, s_len // tq),
        in_specs=[
            pl.BlockSpec((tq, hw), lambda h, i: (i, h)),
            pl.BlockSpec((s_len, hw), lambda h, i: (0, H_A + h)),
            pl.BlockSpec((s_len, hw), lambda h, i: (0, 2 * H_A + h)),
            pl.BlockSpec((None, 2, tq, tq), lambda h, i: (h, 0, 0, 0)),
            pl.BlockSpec((4, D_QK), lambda h, i: (0, 0)),
            pl.BlockSpec((1, hw), lambda h, i: (0, 0)),
        ],
        out_specs=pl.BlockSpec((tq, hw), lambda h, i: (i, h)),
        out_shape=jax.ShapeDtypeStruct((s_len, w), BF16),
        scratch_shapes=[pltpu.VMEM((2 * tq, LANE), F32), pltpu.VMEM((2 * tq, LANE), F32), pltpu.VMEM((2 * tq, hw), F32)],
        compiler_params=_params("parallel", "arbitrary"),
        name="attn_a_prompt",
    )(zb, zb, zb, bias_tiles, lam_p, subln_g.reshape(1, hw))


def _bias_tiles_a_prompt(rel_bias, tq):
    assert _bucket_np(np.array([tq + 1]))[0] == N_BUCKETS - 1
    i = np.arange(tq)[:, None]
    j = np.arange(tq)[None, :]
    idx = []
    for delta in (0, tq):
        dist = delta + i - j
        idx.append(np.where(dist >= 0, _bucket_np(dist), MASKED))
    idx = np.stack(idx).reshape(1, 2 * tq, tq).astype(np.int32)
    shifted = (rel_bias - rel_bias[N_BUCKETS - 1:]) * np.float32(LOG2E)
    tab = jnp.stack([_head_table(shifted, np.full(tq, h)) for h in range(H_A)])
    return _bias_expand(idx, tab, tr=min(tq, 256)).reshape(H_A, 2, tq, tq)


def _block_diag_queries(q, group_width):
    t = q.shape[0]
    tiled = jnp.concatenate([q] * (LANE // t), axis=0).astype(F32)
    row_g = lax.broadcasted_iota(jnp.int32, tiled.shape, 0) // t
    col_g = lax.broadcasted_iota(jnp.int32, tiled.shape, 1) // group_width
    return jnp.where(row_g == col_g, tiled, 0.0).T.astype(BF16)


def _attn_a_sample_kernel(pt_ref, q_ref, kn_ref, vn_ref, bp_ref, bn_ref, lam_ref, sg_ref, *rest, n_pages, page, layer):
    k_pages = rest[:n_pages]
    v_pages = rest[n_pages:2 * n_pages]
    o_ref = rest[2 * n_pages]
    s_ref, p_ref = rest[2 * n_pages + 1:]
    wq = _block_diag_queries(q_ref[...], D_QK)

    def heads_to_lanes(ref):
        return jnp.concatenate([ref[pl.ds(h, page, stride=H_A), :] for h in range(H_A)], axis=1).astype(BF16)

    for p in range(n_pages):
        s_ref[p * page:(p + 1) * page, :] = _dot(heads_to_lanes(k_pages[p]), wq) + bp_ref[p * page:(p + 1) * page, :]
    s_new = _dot(kn_ref[...], wq) + bn_ref[...]
    s_past = s_ref[...]
    m = jnp.maximum(jnp.max(s_past, axis=0, keepdims=True), jnp.max(s_new, axis=0, keepdims=True))
    e_past = jnp.exp2(s_past - m)
    e_new = jnp.exp2(s_new - m)
    den = jnp.sum(e_past, axis=0, keepdims=True) + jnp.sum(e_new, axis=0, keepdims=True)
    col = lax.broadcasted_iota(jnp.int32, den.shape, 1)
    t_new = kn_ref.shape[0]
    lam = _diff_lambda(lam_ref, layer)
    scale = jnp.where((col // t_new) % 2 == 0, 1.0, -lam) / den
    p_ref[...] = (e_past * scale).astype(BF16)
    acc = _dot_tn((e_new * scale).astype(BF16), vn_ref[...])
    for p in range(n_pages):
        acc = acc + _dot_tn(p_ref[p * page:(p + 1) * page, :], heads_to_lanes(v_pages[p]))
    for h in range(H_A):
        r0 = 2 * h * t_new
        o = acc[r0:r0 + t_new, h * LANE:(h + 1) * LANE] + acc[r0 + t_new:r0 + 2 * t_new, h * LANE:(h + 1) * LANE]
        o_ref[:, h * LANE:(h + 1) * LANE] = (_rms(o, sg_ref[...]) * np.float32(1.0 - _lambda_init(layer))).astype(o_ref.dtype)


def _attn_a_sample(zb_s, cache_k, cache_v, page_table, e, bias_past, bias_new, lam_p, subln_g, layer):
    bsz, t, w3 = zb_s.shape
    w = w3 // 3
    n_pages = page_table.shape[1]
    page = cache_k.shape[2]
    assert 2 * H_A * t == LANE

    cache_k = cache_k.reshape(cache_k.shape[0], cache_k.shape[1], page * H_A, LANE)
    cache_v = cache_v.reshape(cache_v.shape[0], cache_v.shape[1], page * H_A, LANE)

    def page_spec(p):
        return pl.BlockSpec((None, None, page * H_A, LANE), lambda b, pt: (pt[b * n_pages + p], e, 0, 0))

    grid_spec = pltpu.PrefetchScalarGridSpec(
        num_scalar_prefetch=1,
        grid=(bsz,),
        in_specs=[
            pl.BlockSpec((None, t, w), lambda b, pt: (b, 0, 0)),
            pl.BlockSpec((None, t, w), lambda b, pt: (b, 0, 1)),
            pl.BlockSpec((None, t, w), lambda b, pt: (b, 0, 2)),
            pl.BlockSpec((n_pages * page, LANE), lambda b, pt: (0, 0)),
            pl.BlockSpec((t, LANE), lambda b, pt: (0, 0)),
            pl.BlockSpec((4, D_QK), lambda b, pt: (0, 0)),
            pl.BlockSpec((1, LANE), lambda b, pt: (0, 0)),
        ] + [page_spec(p) for p in range(n_pages)] + [page_spec(p) for p in range(n_pages)],
        out_specs=pl.BlockSpec((None, t, w), lambda b, pt: (b, 0, 0)),
        scratch_shapes=[pltpu.VMEM((n_pages * page, LANE), F32), pltpu.VMEM((n_pages * page, LANE), BF16)],
    )
    return pl.pallas_call(
        functools.partial(_attn_a_sample_kernel, n_pages=n_pages, page=page, layer=layer),
        grid_spec=grid_spec,
        out_shape=jax.ShapeDtypeStruct((bsz, t, w), BF16),
        compiler_params=_params("arbitrary"),
        name="attn_a_sample",
    )(page_table.reshape(-1), zb_s, zb_s, zb_s, bias_past, bias_new, lam_p, subln_g.reshape(1, LANE),
      *([cache_k] * n_pages), *([cache_v] * n_pages))


def _bias_a_sample(rel_bias, past, t):
    kpos = np.arange(past + t)[:, None]
    col = np.arange(LANE)[None, :]
    dist = past + (col % t) - kpos
    idx = np.where(dist >= 0, _bucket_np(dist), MASKED).astype(np.int32)
    pad = (-idx.shape[0]) % 8
    idx = np.pad(idx, ((0, pad), (0, 0)), constant_values=MASKED)[None]
    tab = _head_table(rel_bias * np.float32(LOG2E), np.arange(LANE) // (2 * t))[None]
    out = _bias_expand(idx, tab, tr=idx.shape[1])[0]
    return out[:past], out[past:past + t]


def _gate_prompt_kernel(u_ref, g_ref, w_ref, b_ref, o_ref, *, chunks):
    row = lax.broadcasted_iota(jnp.int32, (CHUNK_B, CHUNK_B), 0)
    col = lax.broadcasted_iota(jnp.int32, (CHUNK_B, CHUNK_B), 1)
    for gi in range(G_B):
        wg = jnp.where(row >= col, w_ref[gi], 0.0).astype(BF16)
        bg = b_ref[:, gi:gi + 1]
        for c in range(chunks):
            rs = slice(c * CHUNK_B, (c + 1) * CHUNK_B)
            cs = slice(gi * C_B, (gi + 1) * C_B)
            mixed = _dot(wg, g_ref[rs, cs].astype(BF16)) + bg
            o_ref[rs, cs] = (u_ref[rs, cs] * mixed).astype(o_ref.dtype)


def _gate_prompt(u, g, w_s, b_s, *, chunks=4):
    s_len, w = u.shape
    tm = chunks * CHUNK_B
    return pl.pallas_call(
        functools.partial(_gate_prompt_kernel, chunks=chunks),
        grid=(s_len // tm,),
        in_specs=[
            pl.BlockSpec((tm, w), lambda i: (i, 0)),
            pl.BlockSpec((tm, w), lambda i: (i, 0)),
            pl.BlockSpec((G_B, CHUNK_B, CHUNK_B), lambda i: (0, 0, 0)),
            pl.BlockSpec((CHUNK_B, G_B), lambda i: (0, 0)),
        ],
        out_specs=pl.BlockSpec((tm, w), lambda i: (i, 0)),
        out_shape=jax.ShapeDtypeStruct((s_len, w), BF16),
        compiler_params=_params("parallel"),
        name="gate_prompt",
    )(u, g, w_s, b_s.T)


def _gate_sample_kernel(u_ref, g_ref, w_ref, b_ref, o_ref):
    t = u_ref.shape[1]
    g = g_ref[...]
    mixed = jnp.zeros(g.shape, F32) + b_ref[...][None]
    for s in range(t):
        mixed = mixed + w_ref[:, s, :][None] * g[:, s:s + 1, :]
    o_ref[...] = (u_ref[...] * mixed).astype(o_ref.dtype)


def _gate_sample(u, g, w_s, b_s, *, tb=32):
    bsz, t, w = u.shape
    causal = np.tril(np.ones((t, t), bool))
    wt = jnp.where(jnp.asarray(causal)[None], w_s[:, :t, :t], 0.0)
    wt = jnp.repeat(jnp.transpose(wt, (1, 2, 0)), C_B, axis=2)
    bt = jnp.repeat(b_s[:, :t].T, C_B, axis=1)
    return pl.pallas_call(
        _gate_sample_kernel,
        grid=(bsz // tb,),
        in_specs=[
            pl.BlockSpec((tb, t, w), lambda i: (i, 0, 0)),
            pl.BlockSpec((tb, t, w), lambda i: (i, 0, 0)),
            pl.BlockSpec((t, t, w), lambda i: (0, 0, 0)),
            pl.BlockSpec((t, w), lambda i: (0, 0)),
        ],
        out_specs=pl.BlockSpec((tb, t, w), lambda i: (i, 0, 0)),
        out_shape=jax.ShapeDtypeStruct((bsz, t, w), BF16),
        compiler_params=_params("parallel"),
        name="gate_sample",
    )(u, g, wt, bt)


def _attn_c_prompt_kernel(q_ref, kp_ref, kc_ref, vp_ref, vc_ref, bias_ref, o_ref, ob_ref, lb_ref, *, sb_len):
    sb = pl.program_id(0)

    def rows(ref, start, n, dil):
        return ref[pl.ds(start, n), :] if dil == 1 else ref[pl.ds(start, n, stride=dil), :]

    for bi, (window, dil) in enumerate(DILATED_BRANCHES):
        nstep = window // dil
        span = dil * nstep
        col = lax.broadcasted_iota(jnp.int32, (1, 2 * nstep), 1)
        no_prev = jnp.where(jnp.logical_and(sb == 0, col < nstep), np.float32(NEG_INF), np.float32(0.0))
        for blk in range(sb_len // span):
            base = blk * span
            for r in range(dil):
                q = rows(q_ref, base + r, nstep, dil).astype(BF16)
                if blk == 0:
                    k_prev = rows(kp_ref, sb_len - span + r, nstep, dil)
                    v_prev = rows(vp_ref, sb_len - span + r, nstep, dil)
                else:
                    k_prev = rows(kc_ref, base - span + r, nstep, dil)
                    v_prev = rows(vc_ref, base - span + r, nstep, dil)
                k = jnp.concatenate([k_prev, rows(kc_ref, base + r, nstep, dil)], axis=0).astype(BF16)
                v = jnp.concatenate([v_prev, rows(vc_ref, base + r, nstep, dil)], axis=0).astype(BF16)
                s = _dot_nt(q, k) + bias_ref[bi]
                if blk == 0:
                    s = s + no_prev
                m = jnp.max(s, axis=-1, keepdims=True)
                e = jnp.exp(s - m)
                den = jnp.sum(e, axis=-1, keepdims=True)
                o = _dot(e.astype(BF16), v) / den
                lse = jnp.broadcast_to(m + jnp.log(den), o.shape)
                if dil == 1:
                    ob_ref[bi, pl.ds(base, nstep), :] = o
                    lb_ref[bi, pl.ds(base, nstep), :] = lse
                else:
                    ob_ref[bi, pl.ds(base + r, nstep, stride=dil), :] = o
                    lb_ref[bi, pl.ds(base + r, nstep, stride=dil), :] = lse
    ls = [lb_ref[b] for b in range(len(DILATED_BRANCHES))]
    mm = functools.reduce(jnp.maximum, ls)
    es = [jnp.exp(l - mm) for l in ls]
    den = functools.reduce(lambda a, b: a + b, es)
    out = functools.reduce(lambda a, b: a + b, [(e / den) * ob_ref[b] for b, e in enumerate(es)])
    o_ref[...] = out.astype(o_ref.dtype)


def _attn_c_prompt(zf, s_len, bias):
    d = zf.shape[1] // 3
    nbr = len(DILATED_BRANCHES)
    sb_len = max(window for window, _ in DILATED_BRANCHES)
    assert s_len % sb_len == 0 and all(sb_len % window == 0 for window, _ in DILATED_BRANCHES)
    nstep = bias.shape[2]
    prev = lambda part: (lambda i, h: (jnp.maximum(i - 1, 0), part * H_C + h))
    cur = lambda part: (lambda i, h: (i, part * H_C + h))
    return pl.pallas_call(
        functools.partial(_attn_c_prompt_kernel, sb_len=sb_len),
        grid=(s_len // sb_len, H_C),
        in_specs=[
            pl.BlockSpec((sb_len, D_C), cur(0)),
            pl.BlockSpec((sb_len, D_C), prev(1)),
            pl.BlockSpec((sb_len, D_C), cur(1)),
            pl.BlockSpec((sb_len, D_C), prev(2)),
            pl.BlockSpec((sb_len, D_C), cur(2)),
            pl.BlockSpec((nbr, None, nstep, 2 * nstep), lambda i, h: (0, h, 0, 0)),
        ],
        out_specs=pl.BlockSpec((sb_len, D_C), lambda i, h: (i, h)),
        out_shape=jax.ShapeDtypeStruct((s_len, d), BF16),
        scratch_shapes=[pltpu.VMEM((nbr, sb_len, D_C), F32), pltpu.VMEM((nbr, sb_len, D_C), F32)],
        compiler_params=_params("parallel", "arbitrary"),
        name="attn_c_prompt",
    )(zf, zf, zf, zf, zf, bias)


def _bias_tiles_c_prompt(rel_bias):
    idx = []
    for window, dil in DILATED_BRANCHES:
        nstep = window // dil
        step = nstep + np.arange(nstep)[:, None] - np.arange(2 * nstep)[None, :]
        band = (step >= 0) & (step <= nstep)
        idx.append(np.where(band, _bucket_np(np.clip(step, 0, nstep) * dil), MASKED))
    nstep = idx[0].shape[0]
    assert all(i.shape == (nstep, 2 * nstep) for i in idx)
    nbr = len(idx)
    idx = np.stack(idx).reshape(1, nbr * nstep, 2 * nstep).astype(np.int32)
    tab = jnp.stack([_head_table(rel_bias, np.full(2 * nstep, h)) for h in range(H_C)])
    out = _bias_expand(idx, tab, tr=nstep)
    return jnp.swapaxes(out.reshape(H_C, nbr, nstep, 2 * nstep), 0, 1)


def _attn_c_sample_kernel(q_ref, kn_ref, vn_ref, bp_ref, bn_ref, kh_ref, kf_ref, vh_ref, vf_ref, o_ref,
                          s_ref, p_ref, *, n_half, n_full, ck):
    t_new = kn_ref.shape[0]
    hk = ck // 2
    kept = KEY_GROUP // 2
    gpc = ck // KEY_GROUP

    def full_to_lanes(ref, c):
        return jnp.concatenate([ref[pl.ds(c * ck * H_C + h, ck, stride=H_C), :] for h in range(H_C)],
                               axis=1).astype(BF16)

    def half_to_lanes(ref, c):
        return jnp.concatenate(
            [jnp.concatenate([ref[g, pl.ds(h, kept, stride=H_C), :] for g in range(c * gpc, (c + 1) * gpc)], axis=0)
             for h in range(H_C)], axis=1).astype(BF16)

    chunks = [(half_to_lanes, c, slice(c * hk, (c + 1) * hk)) for c in range(n_half)]
    chunks += [(full_to_lanes, c, slice(n_half * hk + c * ck, n_half * hk + (c + 1) * ck)) for c in range(n_full)]

    wq = _block_diag_queries(q_ref[...], D_C)
    for to_lanes, c, rs in chunks:
        ref = kh_ref if to_lanes is half_to_lanes else kf_ref
        s_ref[rs, :] = _dot(to_lanes(ref, c), wq)
    s_past = s_ref[...]
    s_new = _dot(kn_ref[...], wq)
    es, lses = [], []
    for b in range(bp_ref.shape[0]):
        sp = s_past + bp_ref[b]
        sn = s_new + bn_ref[b]
        m = jnp.maximum(jnp.max(sp, axis=0, keepdims=True), jnp.max(sn, axis=0, keepdims=True))
        ep = jnp.exp(sp - m)
        en = jnp.exp(sn - m)
        den = jnp.sum(ep, axis=0, keepdims=True) + jnp.sum(en, axis=0, keepdims=True)
        es.append((ep, en, den))
        lses.append(m + jnp.log(den))
    mm = functools.reduce(jnp.maximum, lses)
    ws = [jnp.exp(l - mm) for l in lses]
    wsum = functools.reduce(lambda a, b: a + b, ws)
    pp = jnp.zeros(s_past.shape, F32)
    pn = jnp.zeros(s_new.shape, F32)
    for (ep, en, den), w in zip(es, ws):
        coef = (w / wsum) / den
        pp = pp + coef * ep
        pn = pn + coef * en
    p_ref[...] = pp.astype(BF16)
    acc = _dot_tn(pn.astype(BF16), vn_ref[...])
    for to_lanes, c, rs in chunks:
        ref = vh_ref if to_lanes is half_to_lanes else vf_ref
        acc = acc + _dot_tn(p_ref[rs, :], to_lanes(ref, c))
    for h in range(H_C):
        o_ref[:, h * D_C:(h + 1) * D_C] = acc[h * t_new:(h + 1) * t_new, h * D_C:(h + 1) * D_C].astype(o_ref.dtype)


KEY_GROUP = max(dil for _, dil in DILATED_BRANCHES)
CK_C = 512


def _c_sample_plan(wlen, t, ck):
    dist = wlen + np.arange(t)[None, :] - np.arange(wlen)[:, None]
    need = np.zeros(wlen, bool)
    for window, dil in DILATED_BRANCHES:
        need |= ((dist % dil == 0) & (dist <= window)).any(axis=1)
    nck = wlen // ck
    first_half = (np.arange(wlen) % KEY_GROUP) < KEY_GROUP // 2
    n_half = 0
    while n_half < nck and not (need & ~first_half)[n_half * ck:(n_half + 1) * ck].any():
        n_half += 1
    keep = first_half | (np.arange(wlen) >= n_half * ck)
    return n_half, np.nonzero(keep)[0]


def _attn_c_sample(zb_s, cache_k, cache_v, o_idx, bias_past, bias_new, *, ck=CK_C):
    bsz, t, d3 = zb_s.shape
    d = d3 // 3
    wlen = cache_k.shape[2]
    nck = wlen // ck
    nbr = bias_past.shape[0]
    n_half, kidx = _c_sample_plan(wlen, t, ck)
    nrows = len(kidx)
    assert H_C * t == LANE and wlen % ck == 0 and ck % (2 * KEY_GROUP) == 0
    assert 1 <= n_half < nck and bias_past.shape[1] == nrows
    grp = KEY_GROUP * H_C
    full = lambda a: a.reshape(a.shape[0], bsz, wlen * H_C, D_C)
    half = lambda a: a.reshape(a.shape[0], bsz, wlen // KEY_GROUP, grp, D_C)
    n_full = nck - n_half
    assert n_half % n_full == 0
    half_spec = pl.BlockSpec((None, None, n_half * ck // KEY_GROUP, grp // 2, D_C), lambda b: (o_idx, b, 0, 0, 0))
    full_spec = pl.BlockSpec((None, None, n_full * ck * H_C, D_C), lambda b: (o_idx, b, n_half // n_full, 0))
    return pl.pallas_call(
        functools.partial(_attn_c_sample_kernel, n_half=n_half, n_full=n_full, ck=ck),
        grid=(bsz,),
        in_specs=[
            pl.BlockSpec((None, t, d), lambda b: (b, 0, 0)),
            pl.BlockSpec((None, t, d), lambda b: (b, 0, 1)),
            pl.BlockSpec((None, t, d), lambda b: (b, 0, 2)),
            pl.BlockSpec((nbr, nrows, LANE), lambda b: (0, 0, 0)),
            pl.BlockSpec((nbr, t, LANE), lambda b: (0, 0, 0)),
            half_spec, full_spec, half_spec, full_spec,
        ],
        out_specs=pl.BlockSpec((None, t, d), lambda b: (b, 0, 0)),
        out_shape=jax.ShapeDtypeStruct((bsz, t, d), BF16),
        scratch_shapes=[pltpu.VMEM((nrows, LANE), F32), pltpu.VMEM((nrows, LANE), BF16)],
        compiler_params=_params("parallel"),
        name="attn_c_sample",
    )(zb_s, zb_s, zb_s, bias_past, bias_new, half(cache_k), full(cache_k), half(cache_v), full(cache_v))


def _bias_c_sample(rel_bias, wlen, t):
    _, kidx = _c_sample_plan(wlen, t, CK_C)
    nrows = len(kidx)
    krow = np.concatenate([kidx, wlen + np.arange(t)])[:, None]
    col = np.arange(LANE)[None, :]
    dist = wlen + (col % t) - krow
    idx = []
    for window, dil in DILATED_BRANCHES:
        ok = (dist >= 0) & (dist % dil == 0) & (dist <= window)
        idx.append(np.where(ok, _bucket_np(dist), MASKED))
    idx = np.stack(idx).astype(np.int32)
    pad = (-idx.shape[1]) % 8
    idx = np.pad(idx, ((0, 0), (0, pad), (0, 0)), constant_values=MASKED)
    tab = jnp.broadcast_to(_head_table(rel_bias, np.arange(LANE) // t)[None], (idx.shape[0], N_BUCKETS + 1, LANE))
    out = _bias_expand(idx, tab, tr=idx.shape[1])
    return out[:, :nrows], out[:, nrows:nrows + t]


def kernel(x_prompt, x_sample, cache_k_a, cache_v_a, page_table, cache_k_c, cache_v_c, rel_bias, norm_gains, w_ffn_gate, w_ffn_up, w_ffn_down, w_in_ab, w_out_ab, lambda_qk, subln_gain, ln_v_gain, ln_v_bias, w_spatial, b_spatial, w_in_c, w_out_c):
    bp, s_len, d = x_prompt.shape
    bs, t_new, _ = x_sample.shape
    assert bp == 1
    n_s = bs * t_new
    depth = norm_gains.shape[0]
    xp = x_prompt.reshape(s_len, d)
    xs = x_sample.reshape(n_s, d)

    tq_a = 512
    past_a = page_table.shape[1] * cache_k_a.shape[2]
    bias_a_prompt = _bias_tiles_a_prompt(rel_bias, tq_a)
    bias_a_past, bias_a_new = _bias_a_sample(rel_bias, past_a, t_new)
    bias_c_prompt = _bias_tiles_c_prompt(rel_bias)
    bias_c_past, bias_c_new = _bias_c_sample(rel_bias, cache_k_c.shape[2], t_new)

    ka_p, va_p, ka_s, va_s, vb_p, vb_s = [], [], [], [], [], []
    kc_p, vc_p, kc_s, vc_s = [], [], [], []
    for li in range(depth):
        ng = norm_gains[li]

        def half_ffn(xp, xs, j):
            xs, wts = _ffn_half_cast(xs, ng[2 * j, 0], ng[2 * j, 1], w_ffn_gate, w_ffn_up, w_ffn_down, li, j)
            return _ffn_half(xp, ng[2 * j, 0], ng[2 * j, 1], *wts), xs

        xp, xs = half_ffn(xp, xs, 0)
        if li % 2 == 0:
            e = li // 2
            w = H_A * 2 * D_QK
            proj = (ng[1, 0], w_in_ab[e].astype(BF16), ln_v_gain[e], ln_v_bias[e])
            k_p, v_p, u_p, g_p, zb_p = _proj_ab(xp, *proj)
            k_s, v_s, u_s, g_s, zb_s = _proj_ab(xs, *proj)
            ka_p.append(k_p.reshape(1, s_len, H_A, 2 * D_QK))
            va_p.append(v_p.reshape(1, s_len, H_A, 2 * D_QK))
            ka_s.append(k_s.reshape(bs, t_new, H_A, 2 * D_QK))
            va_s.append(v_s.reshape(bs, t_new, H_A, 2 * D_QK))
            last = s_len - ((s_len - 1) // CHUNK_B) * CHUNK_B
            vb_p.append(g_p[s_len - last:].reshape(1, last, w))
            vb_s.append(g_s.reshape(bs, t_new, w))

            o_p = _attn_a_prompt(zb_p, bias_a_prompt, lambda_qk[e], subln_gain[e], li, tq=tq_a)
            o_s = _attn_a_sample(zb_s.reshape(bs, t_new, 3 * w), cache_k_a, cache_v_a, page_table, e,
                                 bias_a_past, bias_a_new, lambda_qk[e], subln_gain[e], li)
            gate_p = _gate_prompt(u_p, g_p, w_spatial[e], b_spatial[e])
            gate_s = _gate_sample(u_s.reshape(bs, t_new, w), g_s.reshape(bs, t_new, w), w_spatial[e], b_spatial[e])
            w_out = w_out_ab[e].astype(BF16)
            xp = _out_proj(o_p, gate_p, w_out, xp, ng[1, 1])
            xs = _out_proj(o_s.reshape(n_s, w), gate_s.reshape(n_s, w), w_out, xs, ng[1, 1])
        else:
            o = li // 2
            w_in = w_in_c[o].astype(BF16)
            zf_p = _proj_c(xp, ng[1, 0], w_in)
            zf_s = _proj_c(xs, ng[1, 0], w_in)
            keep = min(max(wd for wd, _ in DILATED_BRANCHES), s_len)
            kc_p.append(zf_p[s_len - keep:, d:2 * d].reshape(1, keep, H_C, D_C))
            vc_p.append(zf_p[s_len - keep:, 2 * d:].reshape(1, keep, H_C, D_C))
            kc_s.append(zf_s[:, d:2 * d].reshape(bs, t_new, H_C, D_C))
            vc_s.append(zf_s[:, 2 * d:].reshape(bs, t_new, H_C, D_C))
            o_p = _attn_c_prompt(zf_p, s_len, bias_c_prompt)
            o_s = _attn_c_sample(zf_s.astype(BF16).reshape(bs, t_new, 3 * d), cache_k_c, cache_v_c, o,
                                 bias_c_past, bias_c_new).reshape(n_s, d)
            w_out = w_out_c[o].astype(BF16)
            xp = _out_proj(o_p, o_p, w_out, xp, ng[1, 1], a_blk=0, b_blk=1)
            xs = _out_proj(o_s, o_s, w_out, xs, ng[1, 1], a_blk=0, b_blk=1)
        xp, xs = half_ffn(xp, xs, 1)

    y_prompt = xp.reshape(1, s_len, d)
    y_sample = xs.reshape(bs, t_new, d)
    return (y_prompt, y_sample,
            jnp.stack(ka_p, axis=1), jnp.stack(va_p, axis=1), jnp.stack(ka_s, axis=1), jnp.stack(va_s, axis=1),
            jnp.stack(vb_p, axis=0), jnp.stack(vb_s, axis=0),
            jnp.stack(kc_p, axis=0), jnp.stack(vc_p, axis=0), jnp.stack(kc_s, axis=0), jnp.stack(vc_s, axis=0))
```

```python
import functools
import math

import numpy as np
import jax
import jax.numpy as jnp
from jax import lax
from jax.experimental import pallas as pl
from jax.experimental.pallas import tpu as pltpu

F32 = jnp.float32
BF16 = jnp.bfloat16

EPS = 1e-6
NEG_INF = -1e30
LOG2E = math.log2(math.e)
LANE = 128
VMEM_LIMIT = 56 * 1024 * 1024

H_A = 8
D_QK = 64
G_B = 8
C_B = 128
CHUNK_B = 128
H_C = 16
D_C = 128
DILATED_BRANCHES = ((128, 1), (512, 4), (2048, 16))
N_BUCKETS = 32
MAX_DISTANCE = 128
MASKED = N_BUCKETS


def _params(*sem):
    return pltpu.CompilerParams(dimension_semantics=sem, vmem_limit_bytes=VMEM_LIMIT)


def _rms(x, g):
    return x * lax.rsqrt(jnp.mean(x * x, axis=-1, keepdims=True) + EPS) * g


def _gelu(x):
    return 0.5 * x * (1.0 + lax.erf(x * np.float32(math.sqrt(0.5))))


def _dot(a, b):
    return jnp.dot(a, b, preferred_element_type=F32)


def _dot_nt(a, b):
    return lax.dot_general(a, b, (((1,), (1,)), ((), ())), preferred_element_type=F32)


def _dot_tn(a, b):
    return lax.dot_general(a, b, (((0,), (0,)), ((), ())), preferred_element_type=F32)


def _lambda_init(layer):
    return 0.8 - 0.6 * math.exp(-0.3 * layer)


def _diff_lambda(lam_ref, layer):
    lp = lam_ref[...]
    a = jnp.sum(lp[0:1] * lp[1:2], axis=-1, keepdims=True)
    b = jnp.sum(lp[2:3] * lp[3:4], axis=-1, keepdims=True)
    return jnp.exp(a) - jnp.exp(b) + np.float32(_lambda_init(layer))


def _ffn_kernel(x_ref, gpre_ref, gpost_ref, wg_ref, wu_ref, wd_ref, o_ref, h_ref):
    f = pl.program_id(1)

    @pl.when(f == 0)
    def _():
        h_ref[...] = _rms(x_ref[...], gpre_ref[...]).astype(BF16)
        o_ref[...] = jnp.zeros_like(o_ref)

    h = h_ref[...]
    a = _dot(h, wg_ref[...])
    b = _dot(h, wu_ref[...])
    s = (a * jax.nn.sigmoid(a)) * b
    o_ref[...] += _dot(s.astype(BF16), wd_ref[...])

    @pl.when(f == pl.num_programs(1) - 1)
    def _():
        o_ref[...] = x_ref[...] + 0.5 * _rms(o_ref[...], gpost_ref[...])


def _ffn_half(x, g_pre, g_post, wg, wu, wd, *, tm=512, tf=512):
    n, d = x.shape
    f = wg.shape[1]
    return pl.pallas_call(
        _ffn_kernel,
        grid=(n // tm, f // tf),
        in_specs=[
            pl.BlockSpec((tm, d), lambda i, j: (i, 0)),
            pl.BlockSpec((1, d), lambda i, j: (0, 0)),
            pl.BlockSpec((1, d), lambda i, j: (0, 0)),
            pl.BlockSpec((d, tf), lambda i, j: (0, j)),
            pl.BlockSpec((d, tf), lambda i, j: (0, j)),
            pl.BlockSpec((tf, d), lambda i, j: (j, 0)),
        ],
        out_specs=pl.BlockSpec((tm, d), lambda i, j: (i, 0)),
        out_shape=jax.ShapeDtypeStruct((n, d), F32),
        scratch_shapes=[pltpu.VMEM((tm, d), BF16)],
        compiler_params=_params("parallel", "arbitrary"),
        name="ffn_half",
    )(x, g_pre.reshape(1, d), g_post.reshape(1, d), wg, wu, wd)


def _ffn_cast_kernel(x_ref, gpre_ref, gpost_ref, wg_ref, wu_ref, wd_ref, o_ref, wgb_ref, wub_ref, wdb_ref, h_ref):
    wgb_ref[...] = wg_ref[...].astype(BF16)
    wub_ref[...] = wu_ref[...].astype(BF16)
    wdb_ref[...] = wd_ref[...].astype(BF16)
    _ffn_kernel(x_ref, gpre_ref, gpost_ref, wgb_ref, wub_ref, wdb_ref, o_ref, h_ref)


def _ffn_half_cast(x, g_pre, g_post, wg, wu, wd, li, hj, *, tf=256):
    n, d = x.shape
    f = wg.shape[-1]
    once = pl.Buffered(1)
    out, wgb, wub, wdb = pl.pallas_call(
        _ffn_cast_kernel,
        grid=(1, f // tf),
        in_specs=[
            pl.BlockSpec((n, d), lambda i, j: (0, 0), pipeline_mode=once),
            pl.BlockSpec((1, d), lambda i, j: (0, 0)),
            pl.BlockSpec((1, d), lambda i, j: (0, 0)),
            pl.BlockSpec((None, None, d, tf), lambda i, j: (li, hj, 0, j)),
            pl.BlockSpec((None, None, d, tf), lambda i, j: (li, hj, 0, j)),
            pl.BlockSpec((None, None, tf, d), lambda i, j: (li, hj, j, 0)),
        ],
        out_specs=[
            pl.BlockSpec((n, d), lambda i, j: (0, 0)),
            pl.BlockSpec((d, tf), lambda i, j: (0, j)),
            pl.BlockSpec((d, tf), lambda i, j: (0, j)),
            pl.BlockSpec((tf, d), lambda i, j: (j, 0)),
        ],
        out_shape=[jax.ShapeDtypeStruct((n, d), F32), jax.ShapeDtypeStruct((d, f), BF16),
                   jax.ShapeDtypeStruct((d, f), BF16), jax.ShapeDtypeStruct((f, d), BF16)],
        scratch_shapes=[pltpu.VMEM((n, d), BF16)],
        compiler_params=_params("arbitrary", "arbitrary"),
        name="ffn_half_cast",
    )(x, g_pre.reshape(1, d), g_post.reshape(1, d), wg, wu, wd)
    return out, (wgb, wub, wdb)


def _proj_ab_kernel(x_ref, g_ref, w_ref, lng_ref, lnb_ref, k_ref, v_ref, u_ref, gv_ref, zb_ref, h_ref):
    j = pl.program_id(1)

    @pl.when(j == 0)
    def _():
        h_ref[...] = _rms(x_ref[...], g_ref[...]).astype(BF16)

    z = _dot(h_ref[...], w_ref[...])

    @pl.when(j == 0)
    def _():
        zb_ref[...] = (z * np.float32(D_QK ** -0.5 * LOG2E)).astype(BF16)

    @pl.when(j == 1)
    def _():
        k_ref[...] = z
        zb_ref[...] = z.astype(BF16)

    @pl.when(j == 2)
    def _():
        v_ref[...] = z
        zb_ref[...] = z.astype(BF16)

    @pl.when(j == 3)
    def _():
        u_ref[...] = _gelu(z)

    @pl.when(j == 4)
    def _():
        a = _gelu(z)
        c = a - jnp.mean(a, axis=-1, keepdims=True)
        y = c * lax.rsqrt(jnp.mean(c * c, axis=-1, keepdims=True) + EPS)
        gv_ref[...] = y * lng_ref[...] + lnb_ref[...]


def _proj_ab(x, g_pre, w_in, ln_g, ln_b, *, tm=512):
    n, d = x.shape
    w = w_in.shape[1] // 5
    row = pl.BlockSpec((tm, w), lambda i, j: (i, 0))
    return pl.pallas_call(
        _proj_ab_kernel,
        grid=(n // tm, 5),
        in_specs=[
            pl.BlockSpec((tm, d), lambda i, j: (i, 0)),
            pl.BlockSpec((1, d), lambda i, j: (0, 0)),
            pl.BlockSpec((d, w), lambda i, j: (0, j)),
            pl.BlockSpec((1, w), lambda i, j: (0, 0)),
            pl.BlockSpec((1, w), lambda i, j: (0, 0)),
        ],
        out_specs=[row, row, row, row, pl.BlockSpec((tm, w), lambda i, j: (i, jnp.minimum(j, 2)))],
        out_shape=[jax.ShapeDtypeStruct((n, w), F32)] * 4 + [jax.ShapeDtypeStruct((n, 3 * w), BF16)],
        scratch_shapes=[pltpu.VMEM((tm, d), BF16)],
        compiler_params=_params("parallel", "arbitrary"),
        name="proj_ab",
    )(x, g_pre.reshape(1, d), w_in, ln_g.reshape(1, w), ln_b.reshape(1, w))


def _proj_c_kernel(x_ref, g_ref, w_ref, zf_ref, h_ref):
    j = pl.program_id(1)

    @pl.when(j == 0)
    def _():
        h_ref[...] = _rms(x_ref[...], g_ref[...]).astype(BF16)

    z = _dot(h_ref[...], w_ref[...])

    @pl.when(j == 0)
    def _():
        zf_ref[...] = z * np.float32(D_C ** -0.5)

    @pl.when(j > 0)
    def _():
        zf_ref[...] = z


def _proj_c(x, g_pre, w_in, *, tm=512):
    n, d = x.shape
    w = w_in.shape[1] // 3
    return pl.pallas_call(
        _proj_c_kernel,
        grid=(n // tm, 3),
        in_specs=[
            pl.BlockSpec((tm, d), lambda i, j: (i, 0)),
            pl.BlockSpec((1, d), lambda i, j: (0, 0)),
            pl.BlockSpec((d, w), lambda i, j: (0, j)),
        ],
        out_specs=pl.BlockSpec((tm, w), lambda i, j: (i, j)),
        out_shape=jax.ShapeDtypeStruct((n, 3 * w), F32),
        scratch_shapes=[pltpu.VMEM((tm, d), BF16)],
        compiler_params=_params("parallel", "arbitrary"),
        name="proj_c",
    )(x, g_pre.reshape(1, d), w_in)


def _out_proj_kernel(a_ref, b_ref, wa_ref, wb_ref, x_ref, g_ref, o_ref):
    y = _dot(a_ref[...].astype(BF16), wa_ref[...]) + _dot(b_ref[...].astype(BF16), wb_ref[...])
    o_ref[...] = x_ref[...] + _rms(y, g_ref[...])


def _out_proj(a, b, w_out, x, g_post, *, a_blk=0, b_blk=0, tm=512):
    n, d = x.shape
    kh = w_out.shape[0] // 2
    return pl.pallas_call(
        _out_proj_kernel,
        grid=(n // tm,),
        in_specs=[
            pl.BlockSpec((tm, kh), lambda i: (i, a_blk)),
            pl.BlockSpec((tm, kh), lambda i: (i, b_blk)),
            pl.BlockSpec((kh, d), lambda i: (0, 0)),
            pl.BlockSpec((kh, d), lambda i: (1, 0)),
            pl.BlockSpec((tm, d), lambda i: (i, 0)),
            pl.BlockSpec((1, d), lambda i: (0, 0)),
        ],
        out_specs=pl.BlockSpec((tm, d), lambda i: (i, 0)),
        out_shape=jax.ShapeDtypeStruct((n, d), F32),
        compiler_params=_params("parallel"),
        name="out_proj",
    )(a, b, w_out, w_out, x, g_post.reshape(1, d))


def _bucket_np(dist):
    max_exact = N_BUCKETS // 2
    d = np.maximum(dist, 0)
    ratio = np.log(np.maximum(d, 1).astype(np.float32) / np.float32(max_exact)) / np.float32(math.log(MAX_DISTANCE / max_exact))
    large = np.minimum(max_exact + (ratio * (N_BUCKETS - max_exact)).astype(np.int32), N_BUCKETS - 1)
    return np.where(d < max_exact, d, large).astype(np.int32)


def _bias_expand_kernel(idx_ref, tab_ref, o_ref):
    idx = idx_ref[...]
    val = jnp.zeros(idx.shape, F32)
    for b in range(N_BUCKETS + 1):
        val = jnp.where(idx == b, tab_ref[b:b + 1, :], val)
    o_ref[...] = val


def _bias_expand(idx, tab, *, tr=256):
    gi, r, c = idx.shape
    g = tab.shape[0]
    tr = min(tr, r)
    return pl.pallas_call(
        _bias_expand_kernel,
        grid=(g, r // tr),
        in_specs=[
            pl.BlockSpec((None, tr, c), (lambda a, i: (a, i, 0)) if gi == g else (lambda a, i: (0, i, 0))),
            pl.BlockSpec((None, N_BUCKETS + 1, c), lambda a, i: (a, 0, 0)),
        ],
        out_specs=pl.BlockSpec((None, tr, c), lambda a, i: (a, i, 0)),
        out_shape=jax.ShapeDtypeStruct((g, r, c), F32),
        compiler_params=_params("parallel", "parallel"),
        name="bias_expand",
    )(jnp.asarray(idx), tab)


def _head_table(rel_bias, heads_of_col):
    t = rel_bias.astype(F32)[:, np.asarray(heads_of_col)]
    return jnp.concatenate([t, jnp.full((1, t.shape[1]), NEG_INF, F32)], axis=0)


def _attn_a_prompt_kernel(q_ref, k_ref, v_ref, bias_ref, lam_ref, sg_ref, o_ref, m_ref, l_ref, acc_ref, *, tq, layer):
    qi = pl.program_id(1)
    q = q_ref[...]
    lane = lax.broadcasted_iota(jnp.int32, q.shape, 1)
    zero = jnp.zeros_like(q)
    qq = jnp.concatenate([jnp.where(lane < D_QK, q, zero), jnp.where(lane >= D_QK, q, zero)], axis=0)
    m_ref[...] = jnp.full(m_ref.shape, NEG_INF, F32)
    l_ref[...] = jnp.zeros(l_ref.shape, F32)
    acc_ref[...] = jnp.zeros(acc_ref.shape, F32)

    def tile(j, bias_tile, nk=1):
        off = pl.multiple_of(j * tq, tq)
        tk = nk * tq
        s = _dot_nt(qq, k_ref[pl.ds(off, tk), :])
        if bias_tile is not None:
            b = bias_ref[bias_tile]
            s = s + jnp.concatenate([b, b], axis=0)
        st = [s[:, c * LANE:(c + 1) * LANE] for c in range(tk // LANE)]
        m_prev = m_ref[...]
        m_new = jnp.maximum(m_prev, jnp.max(functools.reduce(jnp.maximum, st), axis=-1, keepdims=True))
        alpha = jnp.exp2(m_prev - m_new)
        ps = [jnp.exp2(t - m_new) for t in st]
        l_ref[...] = alpha * l_ref[...] + functools.reduce(lambda a, b: a + b, ps)
        p = jnp.concatenate([x.astype(BF16) for x in ps], axis=1)
        acc_ref[...] = alpha * acc_ref[...] + _dot(p, v_ref[pl.ds(off, tk), :])
        m_ref[...] = m_new

    n_far = jnp.maximum(qi - 1, 0)

    def far_body(j4, carry):
        tile(4 * j4, None, nk=4)
        return carry

    lax.fori_loop(0, n_far // 4, far_body, 0)
    rem = n_far % 4
    done = n_far - rem

    @pl.when(rem >= 2)
    def _():
        tile(done, None, nk=2)

    @pl.when(rem % 2 == 1)
    def _():
        tile(n_far - 1, None)

    @pl.when(qi >= 1)
    def _():
        tile(qi - 1, 1)

    tile(qi, 0)
    lam = _diff_lambda(lam_ref, layer)
    o = acc_ref[...] / jnp.sum(l_ref[...], axis=-1, keepdims=True)
    o = o[:tq] - lam * o[tq:]
    o_ref[...] = (_rms(o, sg_ref[...]) * np.float32(1.0 - _lambda_init(layer))).astype(o_ref.dtype)


def _attn_a_prompt(zb, bias_tiles, lam_p, subln_g, layer, *, tq=512):
    s_len = zb.shape[0]
    w = zb.shape[1] // 3
    hw = w // H_A
    return pl.pallas_call(
        functools.partial(_attn_a_prompt_kernel, tq=tq, layer=layer),
        grid=(H_A, s_len // tq),
        in_specs=[
            pl.BlockSpec((tq, hw), lambda h, i: (i, h)),
            pl.BlockSpec((s_len, hw), lambda h, i: (0, H_A + h)),
            pl.BlockSpec((s_len, hw), lambda h, i: (0, 2 * H_A + h)),
            pl.BlockSpec((None, 2, tq, tq), lambda h, i: (h, 0, 0, 0)),
            pl.BlockSpec((4, D_QK), lambda h, i: (0, 0)),
            pl.BlockSpec((1, hw), lambda h, i: (0, 0)),
        ],
        out_specs=pl.BlockSpec((tq, hw), lambda h, i: (i, h)),
        out_shape=jax.ShapeDtypeStruct((s_len, w), BF16),
        scratch_shapes=[pltpu.VMEM((2 * tq, LANE), F32), pltpu.VMEM((2 * tq, LANE), F32), pltpu.VMEM((2 * tq, hw), F32)],
        compiler_params=_params("parallel", "arbitrary"),
        name="attn_a_prompt",
    )(zb, zb, zb, bias_tiles, lam_p, subln_g.reshape(1, hw))


def _bias_tiles_a_prompt(rel_bias, tq):
    assert _bucket_np(np.array([tq + 1]))[0] == N_BUCKETS - 1
    i = np.arange(tq)[:, None]
    j = np.arange(tq)[None, :]
    idx = []
    for delta in (0, tq):
        dist = delta + i - j
        idx.append(np.where(dist >= 0, _bucket_np(dist), MASKED))
    idx = np.stack(idx).reshape(1, 2 * tq, tq).astype(np.int32)
    shifted = (rel_bias - rel_bias[N_BUCKETS - 1:]) * np.float32(LOG2E)
    tab = jnp.stack([_head_table(shifted, np.full(tq, h)) for h in range(H_A)])
    return _bias_expand(idx, tab, tr=min(tq, 256)).reshape(H_A, 2, tq, tq)


def _block_diag_queries(q, group_width):
    t = q.shape[0]
    tiled = jnp.concatenate([q] * (LANE // t), axis=0).astype(F32)
    row_g = lax.broadcasted_iota(jnp.int32, tiled.shape, 0) // t
    col_g = lax.broadcasted_iota(jnp.int32, tiled.shape, 1) // group_width
    return jnp.where(row_g == col_g, tiled, 0.0).T.astype(BF16)


def _attn_a_sample_kernel(pt_ref, q_ref, kn_ref, vn_ref, bp_ref, bn_ref, lam_ref, sg_ref, *rest, n_pages, page, layer):
    k_pages = rest[:n_pages]
    v_pages = rest[n_pages:2 * n_pages]
    o_ref = rest[2 * n_pages]
    s_ref, p_ref = rest[2 * n_pages + 1:]
    wq = _block_diag_queries(q_ref[...], D_QK)

    def heads_to_lanes(ref):
        return jnp.concatenate([ref[pl.ds(h, page, stride=H_A), :] for h in range(H_A)], axis=1).astype(BF16)

    for p in range(n_pages):
        s_ref[p * page:(p + 1) * page, :] = _dot(heads_to_lanes(k_pages[p]), wq) + bp_ref[p * page:(p + 1) * page, :]
    s_new = _dot(kn_ref[...], wq) + bn_ref[...]
    s_past = s_ref[...]
    m = jnp.maximum(jnp.max(s_past, axis=0, keepdims=True), jnp.max(s_new, axis=0, keepdims=True))
    e_past = jnp.exp2(s_past - m)
    e_new = jnp.exp2(s_new - m)
    den = jnp.sum(e_past, axis=0, keepdims=True) + jnp.sum(e_new, axis=0, keepdims=True)
    col = lax.broadcasted_iota(jnp.int32, den.shape, 1)
    t_new = kn_ref.shape[0]
    lam = _diff_lambda(lam_ref, layer)
    scale = jnp.where((col // t_new) % 2 == 0, 1.0, -lam) / den
    p_ref[...] = (e_past * scale).astype(BF16)
    acc = _dot_tn((e_new * scale).astype(BF16), vn_ref[...])
    for p in range(n_pages):
        acc = acc + _dot_tn(p_ref[p * page:(p + 1) * page, :], heads_to_lanes(v_pages[p]))
    for h in range(H_A):
        r0 = 2 * h * t_new
        o = acc[r0:r0 + t_new, h * LANE:(h + 1) * LANE] + acc[r0 + t_new:r0 + 2 * t_new, h * LANE:(h + 1) * LANE]
        o_ref[:, h * LANE:(h + 1) * LANE] = (_rms(o, sg_ref[...]) * np.float32(1.0 - _lambda_init(layer))).astype(o_ref.dtype)


def _attn_a_sample(zb_s, cache_k, cache_v, page_table, e, bias_past, bias_new, lam_p, subln_g, layer):
    bsz, t, w3 = zb_s.shape
    w = w3 // 3
    n_pages = page_table.shape[1]
    page = cache_k.shape[2]
    assert 2 * H_A * t == LANE

    cache_k = cache_k.reshape(cache_k.shape[0], cache_k.shape[1], page * H_A, LANE)
    cache_v = cache_v.reshape(cache_v.shape[0], cache_v.shape[1], page * H_A, LANE)

    def page_spec(p):
        return pl.BlockSpec((None, None, page * H_A, LANE), lambda b, pt: (pt[b * n_pages + p], e, 0, 0))

    grid_spec = pltpu.PrefetchScalarGridSpec(
        num_scalar_prefetch=1,
        grid=(bsz,),
        in_specs=[
            pl.BlockSpec((None, t, w), lambda b, pt: (b, 0, 0)),
            pl.BlockSpec((None, t, w), lambda b, pt: (b, 0, 1)),
            pl.BlockSpec((None, t, w), lambda b, pt: (b, 0, 2)),
            pl.BlockSpec((n_pages * page, LANE), lambda b, pt: (0, 0)),
            pl.BlockSpec((t, LANE), lambda b, pt: (0, 0)),
            pl.BlockSpec((4, D_QK), lambda b, pt: (0, 0)),
            pl.BlockSpec((1, LANE), lambda b, pt: (0, 0)),
        ] + [page_spec(p) for p in range(n_pages)] + [page_spec(p) for p in range(n_pages)],
        out_specs=pl.BlockSpec((None, t, w), lambda b, pt: (b, 0, 0)),
        scratch_shapes=[pltpu.VMEM((n_pages * page, LANE), F32), pltpu.VMEM((n_pages * page, LANE), BF16)],
    )
    return pl.pallas_call(
        functools.partial(_attn_a_sample_kernel, n_pages=n_pages, page=page, layer=layer),
        grid_spec=grid_spec,
        out_shape=jax.ShapeDtypeStruct((bsz, t, w), BF16),
        compiler_params=_params("arbitrary"),
        name="attn_a_sample",
    )(page_table.reshape(-1), zb_s, zb_s, zb_s, bias_past, bias_new, lam_p, subln_g.reshape(1, LANE),
      *([cache_k] * n_pages), *([cache_v] * n_pages))


def _bias_a_sample(rel_bias, past, t):
    kpos = np.arange(past + t)[:, None]
    col = np.arange(LANE)[None, :]
    dist = past + (col % t) - kpos
    idx = np.where(dist >= 0, _bucket_np(dist), MASKED).astype(np.int32)
    pad = (-idx.shape[0]) % 8
    idx = np.pad(idx, ((0, pad), (0, 0)), constant_values=MASKED)[None]
    tab = _head_table(rel_bias * np.float32(LOG2E), np.arange(LANE) // (2 * t))[None]
    out = _bias_expand(idx, tab, tr=idx.shape[1])[0]
    return out[:past], out[past:past + t]


def _gate_prompt_kernel(u_ref, g_ref, w_ref, b_ref, o_ref, *, chunks):
    row = lax.broadcasted_iota(jnp.int32, (CHUNK_B, CHUNK_B), 0)
    col = lax.broadcasted_iota(jnp.int32, (CHUNK_B, CHUNK_B), 1)
    for gi in range(G_B):
        wg = jnp.where(row >= col, w_ref[gi], 0.0).astype(BF16)
        bg = b_ref[:, gi:gi + 1]
        for c in range(chunks):
            rs = slice(c * CHUNK_B, (c + 1) * CHUNK_B)
            cs = slice(gi * C_B, (gi + 1) * C_B)
            mixed = _dot(wg, g_ref[rs, cs].astype(BF16)) + bg
            o_ref[rs, cs] = (u_ref[rs, cs] * mixed).astype(o_ref.dtype)


def _gate_prompt(u, g, w_s, b_s, *, chunks=4):
    s_len, w = u.shape
    tm = chunks * CHUNK_B
    return pl.pallas_call(
        functools.partial(_gate_prompt_kernel, chunks=chunks),
        grid=(s_len // tm,),
        in_specs=[
            pl.BlockSpec((tm, w), lambda i: (i, 0)),
            pl.BlockSpec((tm, w), lambda i: (i, 0)),
            pl.BlockSpec((G_B, CHUNK_B, CHUNK_B), lambda i: (0, 0, 0)),
            pl.BlockSpec((CHUNK_B, G_B), lambda i: (0, 0)),
        ],
        out_specs=pl.BlockSpec((tm, w), lambda i: (i, 0)),
        out_shape=jax.ShapeDtypeStruct((s_len, w), BF16),
        compiler_params=_params("parallel"),
        name="gate_prompt",
    )(u, g, w_s, b_s.T)


def _gate_sample_kernel(u_ref, g_ref, w_ref, b_ref, o_ref):
    t = u_ref.shape[1]
    g = g_ref[...]
    mixed = jnp.zeros(g.shape, F32) + b_ref[...][None]
    for s in range(t):
        mixed = mixed + w_ref[:, s, :][None] * g[:, s:s + 1, :]
    o_ref[...] = (u_ref[...] * mixed).astype(o_ref.dtype)


def _gate_sample(u, g, w_s, b_s, *, tb=32):
    bsz, t, w = u.shape
    causal = np.tril(np.ones((t, t), bool))
    wt = jnp.where(jnp.asarray(causal)[None], w_s[:, :t, :t], 0.0)
    wt = jnp.repeat(jnp.transpose(wt, (1, 2, 0)), C_B, axis=2)
    bt = jnp.repeat(b_s[:, :t].T, C_B, axis=1)
    return pl.pallas_call(
        _gate_sample_kernel,
        grid=(bsz // tb,),
        in_specs=[
            pl.BlockSpec((tb, t, w), lambda i: (i, 0, 0)),
            pl.BlockSpec((tb, t, w), lambda i: (i, 0, 0)),
            pl.BlockSpec((t, t, w), lambda i: (0, 0, 0)),
            pl.BlockSpec((t, w), lambda i: (0, 0)),
        ],
        out_specs=pl.BlockSpec((tb, t, w), lambda i: (i, 0, 0)),
        out_shape=jax.ShapeDtypeStruct((bsz, t, w), BF16),
        compiler_params=_params("parallel"),
        name="gate_sample",
    )(u, g, wt, bt)


def _attn_c_prompt_kernel(q_ref, kp_ref, kc_ref, vp_ref, vc_ref, bias_ref, o_ref, ob_ref, lb_ref, *, sb_len):
    sb = pl.program_id(0)

    def rows(ref, start, n, dil):
        return ref[pl.ds(start, n), :] if dil == 1 else ref[pl.ds(start, n, stride=dil), :]

    for bi, (window, dil) in enumerate(DILATED_BRANCHES):
        nstep = window // dil
        span = dil * nstep
        col = lax.broadcasted_iota(jnp.int32, (1, 2 * nstep), 1)
        no_prev = jnp.where(jnp.logical_and(sb == 0, col < nstep), np.float32(NEG_INF), np.float32(0.0))
        for blk in range(sb_len // span):
            base = blk * span
            for r in range(dil):
                q = rows(q_ref, base + r, nstep, dil).astype(BF16)
                if blk == 0:
                    k_prev = rows(kp_ref, sb_len - span + r, nstep, dil)
                    v_prev = rows(vp_ref, sb_len - span + r, nstep, dil)
                else:
                    k_prev = rows(kc_ref, base - span + r, nstep, dil)
                    v_prev = rows(vc_ref, base - span + r, nstep, dil)
                k = jnp.concatenate([k_prev, rows(kc_ref, base + r, nstep, dil)], axis=0).astype(BF16)
                v = jnp.concatenate([v_prev, rows(vc_ref, base + r, nstep, dil)], axis=0).astype(BF16)
                s = _dot_nt(q, k) + bias_ref[bi]
                if blk == 0:
                    s = s + no_prev
                m = jnp.max(s, axis=-1, keepdims=True)
                e = jnp.exp(s - m)
                den = jnp.sum(e, axis=-1, keepdims=True)
                o = _dot(e.astype(BF16), v) / den
                lse = jnp.broadcast_to(m + jnp.log(den), o.shape)
                if dil == 1:
                    ob_ref[bi, pl.ds(base, nstep), :] = o
                    lb_ref[bi, pl.ds(base, nstep), :] = lse
                else:
                    ob_ref[bi, pl.ds(base + r, nstep, stride=dil), :] = o
                    lb_ref[bi, pl.ds(base + r, nstep, stride=dil), :] = lse
    ls = [lb_ref[b] for b in range(len(DILATED_BRANCHES))]
    mm = functools.reduce(jnp.maximum, ls)
    es = [jnp.exp(l - mm) for l in ls]
    den = functools.reduce(lambda a, b: a + b, es)
    out = functools.reduce(lambda a, b: a + b, [(e / den) * ob_ref[b] for b, e in enumerate(es)])
    o_ref[...] = out.astype(o_ref.dtype)


def _attn_c_prompt(zf, s_len, bias):
    d = zf.shape[1] // 3
    nbr = len(DILATED_BRANCHES)
    sb_len = max(window for window, _ in DILATED_BRANCHES)
    assert s_len % sb_len == 0 and all(sb_len % window == 0 for window, _ in DILATED_BRANCHES)
    nstep = bias.shape[2]
    prev = lambda part: (lambda i, h: (jnp.maximum(i - 1, 0), part * H_C + h))
    cur = lambda part: (lambda i, h: (i, part * H_C + h))
    return pl.pallas_call(
        functools.partial(_attn_c_prompt_kernel, sb_len=sb_len),
        grid=(s_len // sb_len, H_C),
        in_specs=[
            pl.BlockSpec((sb_len, D_C), cur(0)),
            pl.BlockSpec((sb_len, D_C), prev(1)),
            pl.BlockSpec((sb_len, D_C), cur(1)),
            pl.BlockSpec((sb_len, D_C), prev(2)),
            pl.BlockSpec((sb_len, D_C), cur(2)),
            pl.BlockSpec((nbr, None, nstep, 2 * nstep), lambda i, h: (0, h, 0, 0)),
        ],
        out_specs=pl.BlockSpec((sb_len, D_C), lambda i, h: (i, h)),
        out_shape=jax.ShapeDtypeStruct((s_len, d), BF16),
        scratch_shapes=[pltpu.VMEM((nbr, sb_len, D_C), F32), pltpu.VMEM((nbr, sb_len, D_C), F32)],
        compiler_params=_params("parallel", "arbitrary"),
        name="attn_c_prompt",
    )(zf, zf, zf, zf, zf, bias)


def _bias_tiles_c_prompt(rel_bias):
    idx = []
    for window, dil in DILATED_BRANCHES:
        nstep = window // dil
        step = nstep + np.arange(nstep)[:, None] - np.arange(2 * nstep)[None, :]
        band = (step >= 0) & (step <= nstep)
        idx.append(np.where(band, _bucket_np(np.clip(step, 0, nstep) * dil), MASKED))
    nstep = idx[0].shape[0]
    assert all(i.shape == (nstep, 2 * nstep) for i in idx)
    nbr = len(idx)
    idx = np.stack(idx).reshape(1, nbr * nstep, 2 * nstep).astype(np.int32)
    tab = jnp.stack([_head_table(rel_bias, np.full(2 * nstep, h)) for h in range(H_C)])
    out = _bias_expand(idx, tab, tr=nstep)
    return jnp.swapaxes(out.reshape(H_C, nbr, nstep, 2 * nstep), 0, 1)


def _attn_c_sample_kernel(q_ref, kn_ref, vn_ref, bp_ref, bn_ref, kh_ref, kf_ref, vh_ref, vf_ref, o_ref,
                          s_ref, p_ref, *, n_half, n_full, ck):
    t_new = kn_ref.shape[0]
    hk = ck // 2
    kept = KEY_GROUP // 2
    gpc = ck // KEY_GROUP

    def full_to_lanes(ref, c):
        return jnp.concatenate([ref[pl.ds(c * ck * H_C + h, ck, stride=H_C), :] for h in range(H_C)],
                               axis=1).astype(BF16)

    def half_to_lanes(ref, c):
        return jnp.concatenate(
            [jnp.concatenate([ref[g, pl.ds(h, kept, stride=H_C), :] for g in range(c * gpc, (c + 1) * gpc)], axis=0)
             for h in range(H_C)], axis=1).astype(BF16)

    chunks = [(half_to_lanes, c, slice(c * hk, (c + 1) * hk)) for c in range(n_half)]
    chunks += [(full_to_lanes, c, slice(n_half * hk + c * ck, n_half * hk + (c + 1) * ck)) for c in range(n_full)]

    wq = _block_diag_queries(q_ref[...], D_C)
    for to_lanes, c, rs in chunks:
        ref = kh_ref if to_lanes is half_to_lanes else kf_ref
        s_ref[rs, :] = _dot(to_lanes(ref, c), wq)
    s_past = s_ref[...]
    s_new = _dot(kn_ref[...], wq)
    es, lses = [], []
    for b in range(bp_ref.shape[0]):
        sp = s_past + bp_ref[b]
        sn = s_new + bn_ref[b]
        m = jnp.maximum(jnp.max(sp, axis=0, keepdims=True), jnp.max(sn, axis=0, keepdims=True))
        ep = jnp.exp(sp - m)
        en = jnp.exp(sn - m)
        den = jnp.sum(ep, axis=0, keepdims=True) + jnp.sum(en, axis=0, keepdims=True)
        es.append((ep, en, den))
        lses.append(m + jnp.log(den))
    mm = functools.reduce(jnp.maximum, lses)
    ws = [jnp.exp(l - mm) for l in lses]
    wsum = functools.reduce(lambda a, b: a + b, ws)
    pp = jnp.zeros(s_past.shape, F32)
    pn = jnp.zeros(s_new.shape, F32)
    for (ep, en, den), w in zip(es, ws):
        coef = (w / wsum) / den
        pp = pp + coef * ep
        pn = pn + coef * en
    p_ref[...] = pp.astype(BF16)
    acc = _dot_tn(pn.astype(BF16), vn_ref[...])
    for to_lanes, c, rs in chunks:
        ref = vh_ref if to_lanes is half_to_lanes else vf_ref
        acc = acc + _dot_tn(p_ref[rs, :], to_lanes(ref, c))
    for h in range(H_C):
        o_ref[:, h * D_C:(h + 1) * D_C] = acc[h * t_new:(h + 1) * t_new, h * D_C:(h + 1) * D_C].astype(o_ref.dtype)


KEY_GROUP = max(dil for _, dil in DILATED_BRANCHES)
CK_C = 512


def _c_sample_plan(wlen, t, ck):
    dist = wlen + np.arange(t)[None, :] - np.arange(wlen)[:, None]
    need = np.zeros(wlen, bool)
    for window, dil in DILATED_BRANCHES:
        need |= ((dist % dil == 0) & (dist <= window)).any(axis=1)
    nck = wlen // ck
    first_half = (np.arange(wlen) % KEY_GROUP) < KEY_GROUP // 2
    n_half = 0
    while n_half < nck and not (need & ~first_half)[n_half * ck:(n_half + 1) * ck].any():
        n_half += 1
    keep = first_half | (np.arange(wlen) >= n_half * ck)
    return n_half, np.nonzero(keep)[0]


def _attn_c_sample(zb_s, cache_k, cache_v, o_idx, bias_past, bias_new, *, ck=CK_C):
    bsz, t, d3 = zb_s.shape
    d = d3 // 3
    wlen = cache_k.shape[2]
    nck = wlen // ck
    nbr = bias_past.shape[0]
    n_half, kidx = _c_sample_plan(wlen, t, ck)
    nrows = len(kidx)
    assert H_C * t == LANE and wlen % ck == 0 and ck % (2 * KEY_GROUP) == 0
    assert 1 <= n_half < nck and bias_past.shape[1] == nrows
    grp = KEY_GROUP * H_C
    full = lambda a: a.reshape(a.shape[0], bsz, wlen * H_C, D_C)
    half = lambda a: a.reshape(a.shape[0], bsz, wlen // KEY_GROUP, grp, D_C)
    n_full = nck - n_half
    assert n_half % n_full == 0
    half_spec = pl.BlockSpec((None, None, n_half * ck // KEY_GROUP, grp // 2, D_C), lambda b: (o_idx, b, 0, 0, 0))
    full_spec = pl.BlockSpec((None, None, n_full * ck * H_C, D_C), lambda b: (o_idx, b, n_half // n_full, 0))
    return pl.pallas_call(
        functools.partial(_attn_c_sample_kernel, n_half=n_half, n_full=n_full, ck=ck),
        grid=(bsz,),
        in_specs=[
            pl.BlockSpec((None, t, d), lambda b: (b, 0, 0)),
            pl.BlockSpec((None, t, d), lambda b: (b, 0, 1)),
            pl.BlockSpec((None, t, d), lambda b: (b, 0, 2)),
            pl.BlockSpec((nbr, nrows, LANE), lambda b: (0, 0, 0)),
            pl.BlockSpec((nbr, t, LANE), lambda b: (0, 0, 0)),
            half_spec, full_spec, half_spec, full_spec,
        ],
        out_specs=pl.BlockSpec((None, t, d), lambda b: (b, 0, 0)),
        out_shape=jax.ShapeDtypeStruct((bsz, t, d), BF16),
        scratch_shapes=[pltpu.VMEM((nrows, LANE), F32), pltpu.VMEM((nrows, LANE), BF16)],
        compiler_params=_params("parallel"),
        name="attn_c_sample",
    )(zb_s, zb_s, zb_s, bias_past, bias_new, half(cache_k), full(cache_k), half(cache_v), full(cache_v))


def _bias_c_sample(rel_bias, wlen, t):
    _, kidx = _c_sample_plan(wlen, t, CK_C)
    nrows = len(kidx)
    krow = np.concatenate([kidx, wlen + np.arange(t)])[:, None]
    col = np.arange(LANE)[None, :]
    dist = wlen + (col % t) - krow
    idx = []
    for window, dil in DILATED_BRANCHES:
        ok = (dist >= 0) & (dist % dil == 0) & (dist <= window)
        idx.append(np.where(ok, _bucket_np(dist), MASKED))
    idx = np.stack(idx).astype(np.int32)
    pad = (-idx.shape[1]) % 8
    idx = np.pad(idx, ((0, 0), (0, pad), (0, 0)), constant_values=MASKED)
    tab = jnp.broadcast_to(_head_table(rel_bias, np.arange(LANE) // t)[None], (idx.shape[0], N_BUCKETS + 1, LANE))
    out = _bias_expand(idx, tab, tr=idx.shape[1])
    return out[:, :nrows], out[:, nrows:nrows + t]


def kernel(x_prompt, x_sample, cache_k_a, cache_v_a, page_table, cache_k_c, cache_v_c, rel_bias, norm_gains, w_ffn_gate, w_ffn_up, w_ffn_down, w_in_ab, w_out_ab, lambda_qk, subln_gain, ln_v_gain, ln_v_bias, w_spatial, b_spatial, w_in_c, w_out_c):
    bp, s_len, d = x_prompt.shape
    bs, t_new, _ = x_sample.shape
    assert bp == 1
    n_s = bs * t_new
    depth = norm_gains.shape[0]
    xp = x_prompt.reshape(s_len, d)
    xs = x_sample.reshape(n_s, d)

    tq_a = 512
    past_a = page_table.shape[1] * cache_k_a.shape[2]
    bias_a_prompt = _bias_tiles_a_prompt(rel_bias, tq_a)
    bias_a_past, bias_a_new = _bias_a_sample(rel_bias, past_a, t_new)
    bias_c_prompt = _bias_tiles_c_prompt(rel_bias)
    bias_c_past, bias_c_new = _bias_c_sample(rel_bias, cache_k_c.shape[2], t_new)

    ka_p, va_p, ka_s, va_s, vb_p, vb_s = [], [], [], [], [], []
    kc_p, vc_p, kc_s, vc_s = [], [], [], []
    for li in range(depth):
        ng = norm_gains[li]

        def half_ffn(xp, xs, j):
            xs, wts = _ffn_half_cast(xs, ng[2 * j, 0], ng[2 * j, 1], w_ffn_gate, w_ffn_up, w_ffn_down, li, j)
            return _ffn_half(xp, ng[2 * j, 0], ng[2 * j, 1], *wts), xs

        xp, xs = half_ffn(xp, xs, 0)
        if li % 2 == 0:
            e = li // 2
            w = H_A * 2 * D_QK
            proj = (ng[1, 0], w_in_ab[e].astype(BF16), ln_v_gain[e], ln_v_bias[e])
            k_p, v_p, u_p, g_p, zb_p = _proj_ab(xp, *proj)
            k_s, v_s, u_s, g_s, zb_s = _proj_ab(xs, *proj)
            ka_p.append(k_p.reshape(1, s_len, H_A, 2 * D_QK))
            va_p.append(v_p.reshape(1, s_len, H_A, 2 * D_QK))
            ka_s.append(k_s.reshape(bs, t_new, H_A, 2 * D_QK))
            va_s.append(v_s.reshape(bs, t_new, H_A, 2 * D_QK))
            last = s_len - ((s_len - 1) // CHUNK_B) * CHUNK_B
            vb_p.append(g_p[s_len - last:].reshape(1, last, w))
            vb_s.append(g_s.reshape(bs, t_new, w))

            o_p = _attn_a_prompt(zb_p, bias_a_prompt, lambda_qk[e], subln_gain[e], li, tq=tq_a)
            o_s = _attn_a_sample(zb_s.reshape(bs, t_new, 3 * w), cache_k_a, cache_v_a, page_table, e,
                                 bias_a_past, bias_a_new, lambda_qk[e], subln_gain[e], li)
            gate_p = _gate_prompt(u_p, g_p, w_spatial[e], b_spatial[e])
            gate_s = _gate_sample(u_s.reshape(bs, t_new, w), g_s.reshape(bs, t_new, w), w_spatial[e], b_spatial[e])
            w_out = w_out_ab[e].astype(BF16)
            xp = _out_proj(o_p, gate_p, w_out, xp, ng[1, 1])
            xs = _out_proj(o_s.reshape(n_s, w), gate_s.reshape(n_s, w), w_out, xs, ng[1, 1])
        else:
            o = li // 2
            w_in = w_in_c[o].astype(BF16)
            zf_p = _proj_c(xp, ng[1, 0], w_in)
            zf_s = _proj_c(xs, ng[1, 0], w_in)
            keep = min(max(wd for wd, _ in DILATED_BRANCHES), s_len)
            kc_p.append(zf_p[s_len - keep:, d:2 * d].reshape(1, keep, H_C, D_C))
            vc_p.append(zf_p[s_len - keep:, 2 * d:].reshape(1, keep, H_C, D_C))
            kc_s.append(zf_s[:, d:2 * d].reshape(bs, t_new, H_C, D_C))
            vc_s.append(zf_s[:, 2 * d:].reshape(bs, t_new, H_C, D_C))
            o_p = _attn_c_prompt(zf_p, s_len, bias_c_prompt)
            o_s = _attn_c_sample(zf_s.astype(BF16).reshape(bs, t_new, 3 * d), cache_k_c, cache_v_c, o,
                                 bias_c_past, bias_c_new).reshape(n_s, d)
            w_out = w_out_c[o].astype(BF16)
            xp = _out_proj(o_p, o_p, w_out, xp, ng[1, 1], a_blk=0, b_blk=1)
            xs = _out_proj(o_s, o_s, w_out, xs, ng[1, 1], a_blk=0, b_blk=1)
        xp, xs = half_ffn(xp, xs, 1)

    y_prompt = xp.reshape(1, s_len, d)
    y_sample = xs.reshape(bs, t_new, d)
    return (y_prompt, y_sample,
            jnp.stack(ka_p, axis=1), jnp.stack(va_p, axis=1), jnp.stack(ka_s, axis=1), jnp.stack(va_s, axis=1),
            jnp.stack(vb_p, axis=0), jnp.stack(vb_s, axis=0),
            jnp.stack(kc_p, axis=0), jnp.stack(vc_p, axis=0), jnp.stack(kc_s, axis=0), jnp.stack(vc_s, axis=0))
```
